```python
import jax
import jax.numpy as jnp
from jax import lax
import numpy as np

D_MODEL = 1024
BATCH = 8
SEQ = 2048
DEPTH = 4

N_A_LAYERS = DEPTH // 2
N_B_LAYERS = DEPTH - N_A_LAYERS
EPS = 1e-6
D_RNN = D_MODEL
RG_HEADS = 4
RG_BLOCK = D_RNN // RG_HEADS
CONV_W = 4
RG_C = 8.0
HEAD_DIM = 128
N_HEADS = D_MODEL // HEAD_DIM
D_ATT = N_HEADS * HEAD_DIM
MOBA_BLOCK = 256
MOBA_TOPK = 3
Q_CHUNK = 16
NEG_INF = -1e30

kernel_name = 'hawk_moba_yoco_hybrid'


def _rmsnorm(x, g):
    xf = x.astype(jnp.float32)
    y = xf * lax.rsqrt(jnp.mean(xf * xf, axis=-1, keepdims=True) + EPS)
    return (y * g.astype(jnp.float32)).astype(x.dtype)


def _modulate(h, shift, scale):
    return h * (1.0 + scale[:, None, :]) + shift[:, None, :]


def _lin_combine(left, right):
    a_l, b_l = left
    a_r, b_r = right
    return a_l * a_r, a_r * b_l + b_r


def _block_diag(u, w):
    b, s, _ = u.shape
    ub = u.reshape(b, s, RG_HEADS, RG_BLOCK)
    return jnp.einsum('bshi,hij->bshj', ub, w).reshape(b, s, D_RNN)


def _rglru_mixer(h, w_in, conv_w, conv_b, w_a, b_a, w_x, b_x, lam, w_out):
    u, g = jnp.split(h @ w_in, 2, axis=-1)
    u = lax.conv_general_dilated(
        u, conv_w[:, None, :], window_strides=(1,), padding=[(CONV_W - 1, 0)],
        dimension_numbers=('NWC', 'WIO', 'NWC'), feature_group_count=D_RNN) + conv_b
    r = jax.nn.sigmoid((_block_diag(u, w_a) + b_a).astype(jnp.float32))
    gi = jax.nn.sigmoid((_block_diag(u, w_x) + b_x).astype(jnp.float32))
    log_a = -RG_C * r * jax.nn.softplus(-lam.astype(jnp.float32))
    a = jnp.exp(log_a)
    b_in = jnp.sqrt(-jnp.expm1(2.0 * log_a)) * (gi * u.astype(jnp.float32))
    _, hs = lax.associative_scan(_lin_combine, (a, b_in), axis=1)
    y = hs.astype(h.dtype) * jax.nn.silu(g)
    return y @ w_out


def _shared_kv(x, cs, kv_norm_g, kv_mod_w, kv_mod_b, w_kv):
    b, s, _ = x.shape
    shift, scale = jnp.split(cs @ kv_mod_w + kv_mod_b, 2, axis=-1)
    h = _modulate(_rmsnorm(x, kv_norm_g), shift, scale)
    k, v = jnp.split(h @ w_kv, 2, axis=-1)
    k = k.reshape(b, s, N_HEADS, HEAD_DIM).transpose(0, 2, 1, 3)
    v = v.reshape(b, s, N_HEADS, HEAD_DIM).transpose(0, 2, 1, 3)
    s_pad = -(-s // MOBA_BLOCK) * MOBA_BLOCK
    pad = ((0, 0), (0, 0), (0, s_pad - s), (0, 0))
    k = jnp.pad(k, pad)
    v = jnp.pad(v, pad)
    n_blk = s_pad // MOBA_BLOCK
    k_mean = k.astype(jnp.float32).reshape(b, N_HEADS, n_blk, MOBA_BLOCK, HEAD_DIM).mean(axis=3)
    return k, v, k_mean.astype(k.dtype)


def _moba_attention(q, k, v, k_mean):
    b, nh, s, dh = q.shape
    n = b * nh
    s_pad = k.shape[2]
    n_blk = s_pad // MOBA_BLOCK
    k_sel = min(MOBA_TOPK, n_blk)
    qf = q.reshape(n, s, dh)
    kf = k.reshape(n, s_pad, dh)
    vf = v.reshape(n, s_pad, dh)
    kb = k.reshape(n * n_blk, MOBA_BLOCK, dh)
    vb = v.reshape(n * n_blk, MOBA_BLOCK, dh)
    km = k_mean.reshape(n, n_blk, dh)
    base = (jnp.arange(n, dtype=jnp.int32) * n_blk)[:, None, None]
    blk_ids = jnp.arange(n_blk, dtype=jnp.int32)
    sm_scale = dh ** -0.5

    def one_chunk(ci):
        q0 = ci * Q_CHUNK
        blk = q0 // MOBA_BLOCK
        qc = lax.dynamic_slice_in_dim(qf, q0, Q_CHUNK, axis=1)
        gate = jnp.einsum('nqd,nbd->nqb', qc, km).astype(jnp.float32)
        gate = jnp.where(blk_ids < blk, gate, NEG_INF)
        _, idx = lax.top_k(gate, k_sel)
        valid = idx < blk
        kg = kb[base + idx]
        vg = vb[base + idx]
        s_sel = jnp.einsum('nqd,nqkld->nqkl', qc, kg).astype(jnp.float32) * sm_scale
        s_sel = jnp.where(valid[..., None], s_sel, NEG_INF).reshape(n, Q_CHUNK, k_sel * MOBA_BLOCK)
        k_own = lax.dynamic_slice_in_dim(kf, blk * MOBA_BLOCK, MOBA_BLOCK, axis=1)
        v_own = lax.dynamic_slice_in_dim(vf, blk * MOBA_BLOCK, MOBA_BLOCK, axis=1)
        s_own = jnp.einsum('nqd,nld->nql', qc, k_own).astype(jnp.float32) * sm_scale
        q_pos = q0 + jnp.arange(Q_CHUNK, dtype=jnp.int32)
        k_pos = blk * MOBA_BLOCK + jnp.arange(MOBA_BLOCK, dtype=jnp.int32)
        s_own = jnp.where(k_pos[None, None, :] <= q_pos[None, :, None], s_own, NEG_INF)
        p = jax.nn.softmax(jnp.concatenate([s_sel, s_own], axis=-1), axis=-1).astype(v.dtype)
        p_sel = p[..., :k_sel * MOBA_BLOCK].reshape(n, Q_CHUNK, k_sel, MOBA_BLOCK)
        p_own = p[..., k_sel * MOBA_BLOCK:]
        return (jnp.einsum('nqkl,nqkld->nqd', p_sel, vg)
                + jnp.einsum('nql,nld->nqd', p_own, v_own))

    out = lax.map(one_chunk, jnp.arange(s // Q_CHUNK, dtype=jnp.int32))
    return out.transpose(1, 0, 2, 3).reshape(b, nh, s, dh)


def _moba_mixer(h, k, v, k_mean, w_in, w_out):
    b, s, _ = h.shape
    q, g = jnp.split(h @ w_in, 2, axis=-1)
    q = q.reshape(b, s, N_HEADS, HEAD_DIM).transpose(0, 2, 1, 3)
    o = _moba_attention(q, k, v, k_mean).transpose(0, 2, 1, 3).reshape(b, s, D_ATT)
    return (o * jax.nn.silu(g)) @ w_out


def setup_inputs(seed: int = 0) -> dict:
    key = jax.random.key(seed)
    ks = jax.random.split(key, 24)
    f32 = jnp.float32

    def nrm(k, shape, scale):
        return scale * jax.random.normal(k, shape, f32)

    a0 = jax.random.uniform(ks[12], (N_A_LAYERS, D_RNN), f32, 0.9, 0.999)
    s0 = a0 ** (1.0 / RG_C)
    return {
        'x': nrm(ks[0], (BATCH, SEQ, D_MODEL), 1.0),
        'c': nrm(ks[1], (BATCH, D_MODEL), 1.0),
        'mod_w': nrm(ks[2], (DEPTH, D_MODEL, 3 * D_MODEL), 0.5 * D_MODEL ** -0.5),
        'mod_b': nrm(ks[3], (DEPTH, 3 * D_MODEL), 0.02),
        'norm_g': 1.0 + nrm(ks[4], (DEPTH, D_MODEL), 0.02),
        'rg_w_in': nrm(ks[5], (N_A_LAYERS, D_MODEL, 2 * D_RNN), D_MODEL ** -0.5),
        'rg_conv_w': nrm(ks[6], (N_A_LAYERS, CONV_W, D_RNN), CONV_W ** -0.5),
        'rg_conv_b': nrm(ks[7], (N_A_LAYERS, D_RNN), 0.01),
        'rg_w_a': nrm(ks[8], (N_A_LAYERS, RG_HEADS, RG_BLOCK, RG_BLOCK), RG_BLOCK ** -0.5),
        'rg_b_a': nrm(ks[9], (N_A_LAYERS, D_RNN), 0.01),
        'rg_w_x': nrm(ks[10], (N_A_LAYERS, RG_HEADS, RG_BLOCK, RG_BLOCK), RG_BLOCK ** -0.5),
        'rg_b_x': nrm(ks[11], (N_A_LAYERS, D_RNN), 0.01),
        'rg_lambda': jnp.log(s0) - jnp.log1p(-s0),
        'rg_w_out': nrm(ks[13], (N_A_LAYERS, D_RNN, D_MODEL), D_RNN ** -0.5),
        'kv_norm_g': 1.0 + nrm(ks[14], (D_MODEL,), 0.02),
        'kv_mod_w': nrm(ks[15], (D_MODEL, 2 * D_MODEL), 0.5 * D_MODEL ** -0.5),
        'kv_mod_b': nrm(ks[16], (2 * D_MODEL,), 0.02),
        'w_kv': nrm(ks[17], (D_MODEL, 2 * D_ATT), D_MODEL ** -0.5),
        'att_w_in': nrm(ks[18], (N_B_LAYERS, D_MODEL, 2 * D_ATT), D_MODEL ** -0.5),
        'att_w_out': nrm(ks[19], (N_B_LAYERS, D_ATT, D_MODEL), D_ATT ** -0.5),
        'final_norm_g': 1.0 + nrm(ks[20], (D_MODEL,), 0.02),
    }


def reference(x, c, mod_w, mod_b, norm_g, rg_w_in, rg_conv_w, rg_conv_b, rg_w_a, rg_b_a,
              rg_w_x, rg_b_x, rg_lambda, rg_w_out, kv_norm_g, kv_mod_w, kv_mod_b, w_kv,
              att_w_in, att_w_out, final_norm_g):
    cs = jax.nn.silu(c)
    k_sh = v_sh = km_sh = None
    for layer in range(DEPTH):
        shift, scale, gate = jnp.split(cs @ mod_w[layer] + mod_b[layer], 3, axis=-1)
        h = _modulate(_rmsnorm(x, norm_g[layer]), shift, scale)
        if layer < N_A_LAYERS:
            i = layer
            y = _rglru_mixer(h, rg_w_in[i], rg_conv_w[i], rg_conv_b[i], rg_w_a[i], rg_b_a[i],
                             rg_w_x[i], rg_b_x[i], rg_lambda[i], rg_w_out[i])
        else:
            if layer == N_A_LAYERS:
                k_sh, v_sh, km_sh = _shared_kv(x, cs, kv_norm_g, kv_mod_w, kv_mod_b, w_kv)
            i = layer - N_A_LAYERS
            y = _moba_mixer(h, k_sh, v_sh, km_sh, att_w_in[i], att_w_out[i])
        x = x + gate[:, None, :] * y
    return _rmsnorm(x, final_norm_g)
```

```python
import functools

import jax
import jax.numpy as jnp
from jax import lax
from jax.experimental import pallas as pl
from jax.experimental.pallas import tpu as pltpu

EPS = 1e-6
RG_C = 8.0
HEAD_DIM = 128
MOBA_BLOCK = 256
MOBA_TOPK = 3
NEG_INF = -1e30
SUBLANES = 8
VMEM_LIMIT_BYTES = 56 * 1024 * 1024

F32 = jnp.float32
BF16 = jnp.bfloat16


def _sigmoid(z):
    return 1.0 / (1.0 + jnp.exp(-z))


def _norm_modulate(x, norm_g, shift, scale):
    ms = jnp.mean(x * x, axis=-1, keepdims=True)
    return x * lax.rsqrt(ms + EPS) * (norm_g * (1.0 + scale)) + shift


def _mod_kernel(c_ref, w_ref, b_ref, o_ref):
    c = c_ref[...]
    cs = (c * _sigmoid(c)).astype(BF16)
    w = w_ref[0].astype(BF16)
    o_ref[0] = jnp.dot(cs, w, preferred_element_type=F32) + b_ref[0]


def _modulation(c, w, b, tn):
    n_layers, d, n = w.shape
    bsz = c.shape[0]
    return pl.pallas_call(
        _mod_kernel,
        grid=(n_layers, n // tn),
        in_specs=[
            pl.BlockSpec((bsz, d), lambda l, j: (0, 0)),
            pl.BlockSpec((1, d, tn), lambda l, j: (l, 0, j)),
            pl.BlockSpec((1, 1, tn), lambda l, j: (l, 0, j)),
        ],
        out_specs=pl.BlockSpec((1, bsz, tn), lambda l, j: (l, 0, j)),
        out_shape=jax.ShapeDtypeStruct((n_layers, bsz, n), F32),
        compiler_params=pltpu.CompilerParams(
            dimension_semantics=("arbitrary", "arbitrary"),
            vmem_limit_bytes=VMEM_LIMIT_BYTES),
        name="adaln_mod",
    )(c, w, b.reshape(n_layers, 1, n))


def _rglru_kernel(x_ref, mod_ref, ng_ref, win_ref, cw_ref, cb_ref, wax_ref, ba_ref,
                  bx_ref, lam_ref, wout_ref, o_ref,
                  ug_s, ubuf, a_s, b_s, hstate):
    tm = x_ref.shape[1]
    d = x_ref.shape[2]
    n_heads, rb, _ = wax_ref.shape
    conv_w = cw_ref.shape[0]

    @pl.when(pl.program_id(1) == 0)
    def _():
        ubuf[0:SUBLANES, :] = jnp.zeros((SUBLANES, d), F32)
        hstate[...] = jnp.zeros_like(hstate)

    x = x_ref[0]
    mod = mod_ref[0]
    shift, scale, gate = mod[:, :d], mod[:, d:2 * d], mod[:, 2 * d:]
    h = _norm_modulate(x, ng_ref[...], shift, scale)
    ug_s[...] = jnp.dot(h.astype(BF16), win_ref[...], preferred_element_type=F32)

    ubuf[SUBLANES:SUBLANES + tm, :] = ug_s[:, :d]
    uc = cb_ref[...] + cw_ref[conv_w - 1:conv_w, :] * ug_s[:, :d]
    for k in range(conv_w - 1):
        back = conv_w - 1 - k
        uc = uc + cw_ref[k:k + 1, :] * ubuf[SUBLANES - back:SUBLANES - back + tm, :]
    ubuf[0:SUBLANES, :] = ubuf[tm:tm + SUBLANES, :]

    lam = lam_ref[...]
    softplus_neg_lam = jnp.maximum(-lam, 0.0) + jnp.log1p(jnp.exp(-jnp.abs(lam)))
    for hh in range(n_heads):
        sl = slice(hh * rb, (hh + 1) * rb)
        uch = uc[:, sl]
        z = jnp.dot(uch.astype(BF16), wax_ref[hh], preferred_element_type=F32)
        r = _sigmoid(z[:, :rb] + ba_ref[:, sl])
        gi = _sigmoid(z[:, rb:] + bx_ref[:, sl])
        log_a = (-RG_C) * r * softplus_neg_lam[:, sl]
        a = jnp.exp(log_a)
        a_s[:, sl] = a
        b_s[:, sl] = jnp.sqrt(1.0 - a * a) * (gi * uch)

    row = lax.broadcasted_iota(jnp.int32, (SUBLANES, d), 0)

    def scan_group(g, h_prev):
        r0 = pl.multiple_of(g * SUBLANES, SUBLANES)
        a = a_s[pl.ds(r0, SUBLANES), :]
        b = b_s[pl.ds(r0, SUBLANES), :]
        step = 1
        while step < SUBLANES:
            keep = row >= step
            a_sh = pltpu.roll(a, step, 0)
            b_sh = pltpu.roll(b, step, 0)
            b = jnp.where(keep, a * b_sh + b, b)
            a = jnp.where(keep, a * a_sh, a)
            step *= 2
        hs = a * h_prev + b
        b_s[pl.ds(r0, SUBLANES), :] = hs
        return hs[SUBLANES - 1:SUBLANES, :]

    hstate[...] = lax.fori_loop(0, tm // SUBLANES, scan_group, hstate[...])

    gpath = ug_s[:, d:]
    y = b_s[...] * (gpath * _sigmoid(gpath))
    o_ref[0] = x + gate * jnp.dot(y.astype(BF16), wout_ref[...], preferred_element_type=F32)


def _rglru_layer(x, mod, norm_g, w_in, conv_w, conv_b, w_ax, b_a, b_x, lam, w_out, tm):
    bsz, seq, d = x.shape
    n_heads, rb, _ = w_ax.shape
    full = lambda shape: pl.BlockSpec(shape, lambda b, i: (0,) * len(shape))
    return pl.pallas_call(
        _rglru_kernel,
        grid=(bsz, seq // tm),
        in_specs=[
            pl.BlockSpec((1, tm, d), lambda b, i: (b, i, 0)),
            pl.BlockSpec((1, 1, 3 * d), lambda b, i: (b, 0, 0)),
            full((1, d)),
            full((d, 2 * d)),
            full(conv_w.shape),
            full((1, d)),
            full((n_heads, rb, 2 * rb)),
            full((1, d)),
            full((1, d)),
            full((1, d)),
            full((d, d)),
        ],
        out_specs=pl.BlockSpec((1, tm, d), lambda b, i: (b, i, 0)),
        out_shape=jax.ShapeDtypeStruct(x.shape, F32),
        scratch_shapes=[
            pltpu.VMEM((tm, 2 * d), F32),
            pltpu.VMEM((tm + SUBLANES, d), F32),
            pltpu.VMEM((tm, d), F32),
            pltpu.VMEM((tm, d), F32),
            pltpu.VMEM((1, d), F32),
        ],
        compiler_params=pltpu.CompilerParams(
            dimension_semantics=("arbitrary", "arbitrary"),
            vmem_limit_bytes=VMEM_LIMIT_BYTES),
        name="rglru_layer",
    )(x, mod, norm_g, w_in, conv_w, conv_b, w_ax, b_a, b_x, lam, w_out)


def _kv_kernel(x_ref, mod_ref, ng_ref, wkv_ref, k_ref, v_ref, km_ref):
    d = x_ref.shape[2]
    n_heads = k_ref.shape[1]
    mod = mod_ref[0]
    h = _norm_modulate(x_ref[0], ng_ref[...], mod[:, :d], mod[:, d:])
    kv = jnp.dot(h.astype(BF16), wkv_ref[...], preferred_element_type=F32)
    d_att = n_heads * HEAD_DIM
    km_ref[0, 0] = jnp.mean(kv[:, :d_att], axis=0, keepdims=True)
    for hd in range(n_heads):
        k_ref[0, hd] = kv[:, hd * HEAD_DIM:(hd + 1) * HEAD_DIM].astype(BF16)
        v_ref[0, hd] = kv[:, d_att + hd * HEAD_DIM:d_att + (hd + 1) * HEAD_DIM].astype(BF16)


def _shared_kv(x, mod, norm_g, w_kv):
    bsz, seq, d = x.shape
    d_att = w_kv.shape[1] // 2
    n_heads = d_att // HEAD_DIM
    n_blk = seq // MOBA_BLOCK
    kv_shape = jax.ShapeDtypeStruct((bsz, n_heads, seq, HEAD_DIM), BF16)
    kv_spec = pl.BlockSpec((1, n_heads, MOBA_BLOCK, HEAD_DIM), lambda b, i: (b, 0, i, 0))
    return pl.pallas_call(
        _kv_kernel,
        grid=(bsz, n_blk),
        in_specs=[
            pl.BlockSpec((1, MOBA_BLOCK, d), lambda b, i: (b, i, 0)),
            pl.BlockSpec((1, 1, 2 * d), lambda b, i: (b, 0, 0)),
            pl.BlockSpec((1, d), lambda b, i: (0, 0)),
            pl.BlockSpec((d, 2 * d_att), lambda b, i: (0, 0)),
        ],
        out_specs=[
            kv_spec, kv_spec,
            pl.BlockSpec((1, 1, 1, d_att), lambda b, i: (b, i, 0, 0)),
        ],
        out_shape=[kv_shape, kv_shape,
                   jax.ShapeDtypeStruct((bsz, n_blk, 1, d_att), F32)],
        compiler_params=pltpu.CompilerParams(
            dimension_semantics=("arbitrary", "arbitrary"),
            vmem_limit_bytes=VMEM_LIMIT_BYTES),
        name="shared_kv",
    )(x, mod, norm_g, w_kv)


def _moba_kernel(x_ref, mod_ref, ng_ref, win_ref, k_ref, v_ref, km_ref, wout_ref, fg_ref,
                 o_ref, q_s, g_s, o_s, m_s, l_s, acc_s, *, apply_final_norm):
    tq = x_ref.shape[1]
    d = x_ref.shape[2]
    n_heads = k_ref.shape[1]
    n_blk = km_ref.shape[2]
    d_att = n_heads * HEAD_DIM
    qb = pl.program_id(1)

    x = x_ref[0]
    mod = mod_ref[0]
    shift, scale, gate = mod[:, :d], mod[:, d:2 * d], mod[:, 2 * d:]
    h = _norm_modulate(x, ng_ref[...], shift, scale)
    qg = jnp.dot(h.astype(BF16), win_ref[...], preferred_element_type=F32)
    for hd in range(n_heads):
        q_s[hd] = qg[:, hd * HEAD_DIM:(hd + 1) * HEAD_DIM]
    g_s[...] = qg[:, d_att:]

    sm_scale = HEAD_DIM ** -0.5
    nt_dims = (((1,), (1,)), ((), ()))
    blk_lane = lax.broadcasted_iota(jnp.int32, (tq, n_blk), 1)
    q_row = lax.broadcasted_iota(jnp.int32, (tq, MOBA_BLOCK), 0)
    k_col = lax.broadcasted_iota(jnp.int32, (tq, MOBA_BLOCK), 1)

    def one_head(hd, carry):
        qf = q_s[hd]
        gsc = lax.dot_general(qf.astype(BF16), km_ref[0, hd].astype(BF16), nt_dims,
                              preferred_element_type=F32)
        gsc = jnp.where(blk_lane < qb, gsc, NEG_INF)
        rank = jnp.zeros((tq, n_blk), jnp.int32)
        for j in range(n_blk):
            gj = gsc[:, j:j + 1]
            ahead = (gj > gsc) | ((gj == gsc) & (blk_lane > j))
            rank = rank + ahead.astype(jnp.int32)
        sel = (rank < MOBA_TOPK) & (blk_lane < qb)

        qh = (qf * sm_scale).astype(BF16)

        r0 = pl.multiple_of(qb * MOBA_BLOCK, MOBA_BLOCK)
        s = lax.dot_general(qh, k_ref[0, hd, pl.ds(r0, MOBA_BLOCK), :], nt_dims,
                            preferred_element_type=F32)
        s = jnp.where(k_col <= q_row, s, NEG_INF)
        m = jnp.max(s, axis=-1, keepdims=True)
        p = jnp.exp(s - m)
        m_s[...] = m
        l_s[...] = jnp.sum(p, axis=-1, keepdims=True)
        acc_s[...] = jnp.dot(p.astype(BF16), v_ref[0, hd, pl.ds(r0, MOBA_BLOCK), :],
                             preferred_element_type=F32)

        for j in range(n_blk - 1):
            @pl.when(j < qb)
            def _(j=j):
                s = lax.dot_general(qh, k_ref[0, hd, j * MOBA_BLOCK:(j + 1) * MOBA_BLOCK, :],
                                    nt_dims, preferred_element_type=F32)
                s = jnp.where(sel[:, j:j + 1], s, NEG_INF)
                m_old = m_s[...]
                m_new = jnp.maximum(m_old, jnp.max(s, axis=-1, keepdims=True))
                alpha = jnp.exp(m_old - m_new)
                p = jnp.exp(s - m_new)
                m_s[...] = m_new
                l_s[...] = alpha * l_s[...] + jnp.sum(p, axis=-1, keepdims=True)
                acc_s[...] = alpha * acc_s[...] + jnp.dot(
                    p.astype(BF16), v_ref[0, hd, j * MOBA_BLOCK:(j + 1) * MOBA_BLOCK, :],
                    preferred_element_type=F32)

        o_s[hd] = acc_s[...] / l_s[...]
        return carry

    lax.fori_loop(0, n_heads, one_head, 0)

    for hd in range(n_heads):
        sl = slice(hd * HEAD_DIM, (hd + 1) * HEAD_DIM)
        gp = g_s[:, sl]
        g_s[:, sl] = o_s[hd] * (gp * _sigmoid(gp))
    out = x + gate * jnp.dot(g_s[...].astype(BF16), wout_ref[...], preferred_element_type=F32)
    if apply_final_norm:
        ms = jnp.mean(out * out, axis=-1, keepdims=True)
        out = out * lax.rsqrt(ms + EPS) * fg_ref[...]
    o_ref[0] = out


def _moba_layer(x, mod, norm_g, w_in, k, v, k_mean, w_out, final_g, apply_final_norm):
    bsz, seq, d = x.shape
    n_heads = k.shape[1]
    d_att = n_heads * HEAD_DIM
    n_blk = seq // MOBA_BLOCK
    full = lambda shape: pl.BlockSpec(shape, lambda b, i: (0,) * len(shape))
    per_batch = lambda shape: pl.BlockSpec(shape, lambda b, i: (b,) + (0,) * (len(shape) - 1))
    return pl.pallas_call(
        functools.partial(_moba_kernel, apply_final_norm=apply_final_norm),
        grid=(bsz, n_blk),
        in_specs=[
            pl.BlockSpec((1, MOBA_BLOCK, d), lambda b, i: (b, i, 0)),
            per_batch((1, 1, 3 * d)),
            full((1, d)),
            full((d, 2 * d_att)),
            per_batch((1, n_heads, seq, HEAD_DIM)),
            per_batch((1, n_heads, seq, HEAD_DIM)),
            per_batch((1, n_heads, n_blk, HEAD_DIM)),
            full((d_att, d)),
            full((1, d)),
        ],
        out_specs=pl.BlockSpec((1, MOBA_BLOCK, d), lambda b, i: (b, i, 0)),
        out_shape=jax.ShapeDtypeStruct(x.shape, F32),
        scratch_shapes=[
            pltpu.VMEM((n_heads, MOBA_BLOCK, HEAD_DIM), F32),
            pltpu.VMEM((MOBA_BLOCK, d_att), F32),
            pltpu.VMEM((n_heads, MOBA_BLOCK, HEAD_DIM), F32),
            pltpu.VMEM((MOBA_BLOCK, 1), F32),
            pltpu.VMEM((MOBA_BLOCK, 1), F32),
            pltpu.VMEM((MOBA_BLOCK, HEAD_DIM), F32),
        ],
        compiler_params=pltpu.CompilerParams(
            dimension_semantics=("arbitrary", "arbitrary"),
            vmem_limit_bytes=VMEM_LIMIT_BYTES),
        name="moba_layer",
    )(x, mod, norm_g, w_in, k, v, k_mean, w_out, final_g)


def kernel(x, c, mod_w, mod_b, norm_g, rg_w_in, rg_conv_w, rg_conv_b, rg_w_a, rg_b_a, rg_w_x,
           rg_b_x, rg_lambda, rg_w_out, kv_norm_g, kv_mod_w, kv_mod_b, w_kv, att_w_in,
           att_w_out, final_norm_g):
    bsz, seq, d = x.shape
    depth = mod_w.shape[0]
    n_a = rg_w_in.shape[0]
    n_b = att_w_in.shape[0]
    assert depth == n_a + n_b and seq % MOBA_BLOCK == 0
    d_att = w_kv.shape[1] // 2
    n_heads = d_att // HEAD_DIM
    n_blk = seq // MOBA_BLOCK

    mod = _modulation(c, mod_w, mod_b, tn=768)
    kv_mod = _modulation(c, kv_mod_w[None], kv_mod_b[None], tn=1024)

    row = lambda p: p.reshape(1, -1)
    for i in range(n_a):
        w_ax = jnp.concatenate([rg_w_a[i], rg_w_x[i]], axis=-1).astype(BF16)
        x = _rglru_layer(
            x, mod[i][:, None, :], row(norm_g[i]), rg_w_in[i].astype(BF16), rg_conv_w[i],
            row(rg_conv_b[i]), w_ax, row(rg_b_a[i]), row(rg_b_x[i]), row(rg_lambda[i]),
            rg_w_out[i].astype(BF16), tm=256)

    k, v, k_mean = _shared_kv(x, kv_mod[0][:, None, :], row(kv_norm_g), w_kv.astype(BF16))
    k_mean = k_mean.reshape(bsz, n_blk, n_heads, HEAD_DIM).transpose(0, 2, 1, 3)

    for i in range(n_b):
        layer = n_a + i
        x = _moba_layer(
            x, mod[layer][:, None, :], row(norm_g[layer]), att_w_in[i].astype(BF16), k, v,
            k_mean, att_w_out[i].astype(BF16), row(final_norm_g),
            apply_final_norm=(i == n_b - 1))
    return x
```

```python
import functools

import jax
import jax.numpy as jnp
from jax import lax
from jax.experimental import pallas as pl
from jax.experimental.pallas import tpu as pltpu

EPS = 1e-6
RG_C = 8.0
HEAD_DIM = 128
MOBA_BLOCK = 256
MOBA_TOPK = 3
NEG_INF = -1e30
LOG2E = 1.4426950408889634
SUBLANES = 8
VMEM_LIMIT_BYTES = 56 * 1024 * 1024

F32 = jnp.float32
BF16 = jnp.bfloat16
NT_DIMS = (((1,), (1,)), ((), ()))


def _sigmoid(z):
    return 1.0 / (1.0 + jnp.exp(-z))


def _norm_modulate(x, norm_g, shift, scale):
    ms = jnp.mean(x * x, axis=-1, keepdims=True)
    return x * lax.rsqrt(ms + EPS) * (norm_g * (1.0 + scale)) + shift


def _mod_kernel(c_ref, w_ref, b_ref, o_ref):
    c = c_ref[...]
    cs = (c * _sigmoid(c)).astype(BF16)
    w = w_ref[0].astype(BF16)
    o_ref[0] = jnp.dot(cs, w, preferred_element_type=F32) + b_ref[0]


def _modulation(c, w, b, tn):
    n_layers, d, n = w.shape
    bsz = c.shape[0]
    return pl.pallas_call(
        _mod_kernel,
        grid=(n_layers, n // tn),
        in_specs=[
            pl.BlockSpec((bsz, d), lambda l, j: (0, 0)),
            pl.BlockSpec((1, d, tn), lambda l, j: (l, 0, j)),
            pl.BlockSpec((1, 1, tn), lambda l, j: (l, 0, j)),
        ],
        out_specs=pl.BlockSpec((1, bsz, tn), lambda l, j: (l, 0, j)),
        out_shape=jax.ShapeDtypeStruct((n_layers, bsz, n), F32),
        compiler_params=pltpu.CompilerParams(
            dimension_semantics=("arbitrary", "arbitrary"),
            vmem_limit_bytes=VMEM_LIMIT_BYTES),
        name="adaln_mod",
    )(c, w, b.reshape(n_layers, 1, n))


def _rglru_kernel(x_ref, mod_ref, ng_ref, win_ref, cw_ref, cb_ref, wax_ref, ba_ref,
                  bx_ref, lam_ref, wout_ref, o_ref,
                  ug_s, ubuf, a_s, b_s, hstate):
    tm = x_ref.shape[1]
    d = x_ref.shape[2]
    n_heads, rb, _ = wax_ref.shape
    conv_w = cw_ref.shape[0]

    @pl.when(pl.program_id(1) == 0)
    def _():
        ubuf[0:SUBLANES, :] = jnp.zeros((SUBLANES, d), F32)
        hstate[...] = jnp.zeros_like(hstate)

    x = x_ref[0]
    mod = mod_ref[0]
    shift, scale, gate = mod[:, :d], mod[:, d:2 * d], mod[:, 2 * d:]
    h = _norm_modulate(x, ng_ref[...], shift, scale)
    ug_s[...] = jnp.dot(h.astype(BF16), win_ref[...], preferred_element_type=F32)

    ubuf[SUBLANES:SUBLANES + tm, :] = ug_s[:, :d]
    uc = cb_ref[...] + cw_ref[conv_w - 1:conv_w, :] * ug_s[:, :d]
    for k in range(conv_w - 1):
        back = conv_w - 1 - k
        uc = uc + cw_ref[k:k + 1, :] * ubuf[SUBLANES - back:SUBLANES - back + tm, :]
    ubuf[0:SUBLANES, :] = ubuf[tm:tm + SUBLANES, :]

    lam = lam_ref[...]
    softplus_neg_lam = jnp.maximum(-lam, 0.0) + jnp.log1p(jnp.exp(-jnp.abs(lam)))
    for hh in range(n_heads):
        sl = slice(hh * rb, (hh + 1) * rb)
        uch = uc[:, sl]
        z = jnp.dot(uch.astype(BF16), wax_ref[hh], preferred_element_type=F32)
        r = _sigmoid(z[:, :rb] + ba_ref[:, sl])
        gi = _sigmoid(z[:, rb:] + bx_ref[:, sl])
        log_a = (-RG_C) * r * softplus_neg_lam[:, sl]
        a = jnp.exp(log_a)
        a_s[:, sl] = a
        b_s[:, sl] = jnp.sqrt(1.0 - a * a) * (gi * uch)

    row = lax.broadcasted_iota(jnp.int32, (SUBLANES, d), 0)

    def scan_group(g, h_prev):
        r0 = pl.multiple_of(g * SUBLANES, SUBLANES)
        a = a_s[pl.ds(r0, SUBLANES), :]
        b = b_s[pl.ds(r0, SUBLANES), :]
        step = 1
        while step < SUBLANES:
            keep = row >= step
            a_sh = pltpu.roll(a, step, 0)
            b_sh = pltpu.roll(b, step, 0)
            b = jnp.where(keep, a * b_sh + b, b)
            a = jnp.where(keep, a * a_sh, a)
            step *= 2
        hs = a * h_prev + b
        b_s[pl.ds(r0, SUBLANES), :] = hs
        return hs[SUBLANES - 1:SUBLANES, :]

    hstate[...] = lax.fori_loop(0, tm // SUBLANES, scan_group, hstate[...])

    gpath = ug_s[:, d:]
    y = b_s[...] * (gpath * _sigmoid(gpath))
    o_ref[0] = x + gate * jnp.dot(y.astype(BF16), wout_ref[...], preferred_element_type=F32)


def _rglru_layer(x, mod, norm_g, w_in, conv_w, conv_b, w_ax, b_a, b_x, lam, w_out, tm):
    bsz, seq, d = x.shape
    n_heads, rb, _ = w_ax.shape
    full = lambda shape: pl.BlockSpec(shape, lambda b, i: (0,) * len(shape))
    return pl.pallas_call(
        _rglru_kernel,
        grid=(bsz, seq // tm),
        in_specs=[
            pl.BlockSpec((1, tm, d), lambda b, i: (b, i, 0)),
            pl.BlockSpec((1, 1, 3 * d), lambda b, i: (b, 0, 0)),
            full((1, d)),
            full((d, 2 * d)),
            full(conv_w.shape),
            full((1, d)),
            full((n_heads, rb, 2 * rb)),
            full((1, d)),
            full((1, d)),
            full((1, d)),
            full((d, d)),
        ],
        out_specs=pl.BlockSpec((1, tm, d), lambda b, i: (b, i, 0)),
        out_shape=jax.ShapeDtypeStruct(x.shape, F32),
        scratch_shapes=[
            pltpu.VMEM((tm, 2 * d), F32),
            pltpu.VMEM((tm + SUBLANES, d), F32),
            pltpu.VMEM((tm, d), F32),
            pltpu.VMEM((tm, d), F32),
            pltpu.VMEM((1, d), F32),
        ],
        compiler_params=pltpu.CompilerParams(
            dimension_semantics=("arbitrary", "arbitrary"),
            vmem_limit_bytes=VMEM_LIMIT_BYTES),
        name="rglru_layer",
    )(x, mod, norm_g, w_in, conv_w, conv_b, w_ax, b_a, b_x, lam, w_out)


def _kv_kernel(x_ref, mod_ref, ng_ref, wk_ref, wvt_ref, k_ref, vt_ref, km_ref):
    d = x_ref.shape[2]
    n_heads = k_ref.shape[1]
    mod = mod_ref[0]
    h = _norm_modulate(x_ref[0], ng_ref[...], mod[:, :d], mod[:, d:]).astype(BF16)
    k = jnp.dot(h, wk_ref[...], preferred_element_type=F32)
    vt = lax.dot_general(wvt_ref[...], h, NT_DIMS, preferred_element_type=F32)
    km_ref[0, 0] = jnp.mean(k, axis=0, keepdims=True)
    for hd in range(n_heads):
        sl = slice(hd * HEAD_DIM, (hd + 1) * HEAD_DIM)
        k_ref[0, hd] = k[:, sl].astype(BF16)
        vt_ref[0, hd] = vt[sl, :].astype(BF16)


def _shared_kv(x, mod, norm_g, w_k, w_v_t):
    bsz, seq, d = x.shape
    d_att = w_k.shape[1]
    n_heads = d_att // HEAD_DIM
    n_blk = seq // MOBA_BLOCK
    return pl.pallas_call(
        _kv_kernel,
        grid=(bsz, n_blk),
        in_specs=[
            pl.BlockSpec((1, MOBA_BLOCK, d), lambda b, i: (b, i, 0)),
            pl.BlockSpec((1, 1, 2 * d), lambda b, i: (b, 0, 0)),
            pl.BlockSpec((1, d), lambda b, i: (0, 0)),
            pl.BlockSpec((d, d_att), lambda b, i: (0, 0)),
            pl.BlockSpec((d_att, d), lambda b, i: (0, 0)),
        ],
        out_specs=[
            pl.BlockSpec((1, n_heads, MOBA_BLOCK, HEAD_DIM), lambda b, i: (b, 0, i, 0)),
            pl.BlockSpec((1, n_heads, HEAD_DIM, MOBA_BLOCK), lambda b, i: (b, 0, 0, i)),
            pl.BlockSpec((1, 1, 1, d_att), lambda b, i: (b, i, 0, 0)),
        ],
        out_shape=[jax.ShapeDtypeStruct((bsz, n_heads, seq, HEAD_DIM), BF16),
                   jax.ShapeDtypeStruct((bsz, n_heads, HEAD_DIM, seq), BF16),
                   jax.ShapeDtypeStruct((bsz, n_blk, 1, d_att), F32)],
        compiler_params=pltpu.CompilerParams(
            dimension_semantics=("arbitrary", "arbitrary"),
            vmem_limit_bytes=VMEM_LIMIT_BYTES),
        name="shared_kv",
    )(x, mod, norm_g, w_k, w_v_t)


def _select_blocks(gate_t, n_past, sel_s):
    blk = lax.broadcasted_iota(jnp.int32, gate_t.shape, 0)
    g = jnp.where(blk < n_past, gate_t, NEG_INF)
    sel_s[...] = g
    rank = jnp.zeros(gate_t.shape, jnp.int32)
    for j in range(n_past):
        gj = jnp.broadcast_to(sel_s[j:j + 1, :], gate_t.shape)
        ahead = (gj > g) | ((gj == g) & (blk > j))
        rank = rank + ahead.astype(jnp.int32)
    sel_s[...] = jnp.where((rank < MOBA_TOPK) & (blk < n_past), 0.0, NEG_INF)


def _attend_head(hd, n_past, q_s, k_ref, vt_ref, km_ref, o_s, sel_s):
    tq = q_s.shape[1]
    qf = q_s[hd]
    qh = (qf * (HEAD_DIM ** -0.5 * LOG2E)).astype(BF16)
    masked = n_past > MOBA_TOPK
    if masked:
        gate_t = lax.dot_general(km_ref[0, hd].astype(BF16), qf.astype(BF16), NT_DIMS,
                                 preferred_element_type=F32)
        _select_blocks(gate_t, n_past, sel_s)

    def fold(a, op):
        return op(a.reshape(a.shape[0] // SUBLANES, SUBLANES, tq), axis=0)

    scores = []
    m8 = None
    for j in range(n_past + 1):
        k_j = k_ref[0, hd, j * MOBA_BLOCK:(j + 1) * MOBA_BLOCK, :]
        s = lax.dot_general(k_j, qh, NT_DIMS, preferred_element_type=F32)
        if j == n_past:
            key = lax.broadcasted_iota(jnp.int32, s.shape, 0)
            qry = lax.broadcasted_iota(jnp.int32, s.shape, 1)
            s = jnp.where(key <= qry, s, NEG_INF)
        elif masked:
            s = s + sel_s[j:j + 1, :]
        scores.append(s)
        smax = fold(s, jnp.max)
        m8 = smax if m8 is None else jnp.maximum(m8, smax)
    m = jnp.max(m8, axis=0, keepdims=True)

    l8 = None
    o_t = None
    for j in range(n_past + 1):
        p = jnp.exp2(scores[j] - m)
        psum = fold(p, jnp.sum)
        l8 = psum if l8 is None else l8 + psum
        pv = jnp.dot(vt_ref[0, hd, :, j * MOBA_BLOCK:(j + 1) * MOBA_BLOCK], p.astype(BF16),
                     preferred_element_type=F32)
        o_t = pv if o_t is None else o_t + pv
    o_t = o_t * (1.0 / jnp.sum(l8, axis=0, keepdims=True))
    o_s[hd] = o_t.T


def _moba_kernel(x_ref, mod_ref, ng_ref, win_ref, k_ref, vt_ref, km_ref, wout_ref, fg_ref,
                 o_ref, q_s, g_s, o_s, sel_s, *, apply_final_norm):
    d = x_ref.shape[2]
    n_heads = k_ref.shape[1]
    n_blk = km_ref.shape[2]
    d_att = n_heads * HEAD_DIM
    qb = pl.program_id(1)

    x = x_ref[0]
    mod = mod_ref[0]
    shift, scale, gate = mod[:, :d], mod[:, d:2 * d], mod[:, 2 * d:]
    h = _norm_modulate(x, ng_ref[...], shift, scale)
    qg = jnp.dot(h.astype(BF16), win_ref[...], preferred_element_type=F32)
    for hd in range(n_heads):
        q_s[hd] = qg[:, hd * HEAD_DIM:(hd + 1) * HEAD_DIM]
    g_s[...] = qg[:, d_att:]

    for n_past in range(n_blk):
        @pl.when(qb == n_past)
        def _(n_past=n_past):
            def one_head(hd, carry):
                _attend_head(hd, n_past, q_s, k_ref, vt_ref, km_ref, o_s, sel_s)
                return carry
            lax.fori_loop(0, n_heads, one_head, 0)

    for hd in range(n_heads):
        sl = slice(hd * HEAD_DIM, (hd + 1) * HEAD_DIM)
        gp = g_s[:, sl]
        g_s[:, sl] = o_s[hd] * (gp * _sigmoid(gp))
    out = x + gate * jnp.dot(g_s[...].astype(BF16), wout_ref[...], preferred_element_type=F32)
    if apply_final_norm:
        ms = jnp.mean(out * out, axis=-1, keepdims=True)
        out = out * lax.rsqrt(ms + EPS) * fg_ref[...]
    o_ref[0] = out


def _moba_layer(x, mod, norm_g, w_in, k, v_t, k_mean, w_out, final_g, apply_final_norm):
    bsz, seq, d = x.shape
    n_heads = k.shape[1]
    d_att = n_heads * HEAD_DIM
    n_blk = seq // MOBA_BLOCK
    full = lambda shape: pl.BlockSpec(shape, lambda b, i: (0,) * len(shape))
    per_batch = lambda shape: pl.BlockSpec(shape, lambda b, i: (b,) + (0,) * (len(shape) - 1))
    return pl.pallas_call(
        functools.partial(_moba_kernel, apply_final_norm=apply_final_norm),
        grid=(bsz, n_blk),
        in_specs=[
            pl.BlockSpec((1, MOBA_BLOCK, d), lambda b, i: (b, i, 0)),
            per_batch((1, 1, 3 * d)),
            full((1, d)),
            full((d, 2 * d_att)),
            per_batch((1, n_heads, seq, HEAD_DIM)),
            per_batch((1, n_heads, HEAD_DIM, seq)),
            per_batch((1, n_heads, n_blk, HEAD_DIM)),
            full((d_att, d)),
            full((1, d)),
        ],
        out_specs=pl.BlockSpec((1, MOBA_BLOCK, d), lambda b, i: (b, i, 0)),
        out_shape=jax.ShapeDtypeStruct(x.shape, F32),
        scratch_shapes=[
            pltpu.VMEM((n_heads, MOBA_BLOCK, HEAD_DIM), F32),
            pltpu.VMEM((MOBA_BLOCK, d_att), F32),
            pltpu.VMEM((n_heads, MOBA_BLOCK, HEAD_DIM), F32),
            pltpu.VMEM((n_blk, MOBA_BLOCK), F32),
        ],
        compiler_params=pltpu.CompilerParams(
            dimension_semantics=("arbitrary", "arbitrary"),
            vmem_limit_bytes=VMEM_LIMIT_BYTES),
        name="moba_layer",
    )(x, mod, norm_g, w_in, k, v_t, k_mean, w_out, final_g)


def kernel(x, c, mod_w, mod_b, norm_g, rg_w_in, rg_conv_w, rg_conv_b, rg_w_a, rg_b_a, rg_w_x,
           rg_b_x, rg_lambda, rg_w_out, kv_norm_g, kv_mod_w, kv_mod_b, w_kv, att_w_in,
           att_w_out, final_norm_g):
    bsz, seq, d = x.shape
    depth = mod_w.shape[0]
    n_a = rg_w_in.shape[0]
    n_b = att_w_in.shape[0]
    assert depth == n_a + n_b and seq % MOBA_BLOCK == 0
    d_att = w_kv.shape[1] // 2
    n_heads = d_att // HEAD_DIM
    n_blk = seq // MOBA_BLOCK

    mod = _modulation(c, mod_w, mod_b, tn=768)
    kv_mod = _modulation(c, kv_mod_w[None], kv_mod_b[None], tn=1024)

    row = lambda p: p.reshape(1, -1)
    for i in range(n_a):
        w_ax = jnp.concatenate([rg_w_a[i], rg_w_x[i]], axis=-1).astype(BF16)
        x = _rglru_layer(
            x, mod[i][:, None, :], row(norm_g[i]), rg_w_in[i].astype(BF16), rg_conv_w[i],
            row(rg_conv_b[i]), w_ax, row(rg_b_a[i]), row(rg_b_x[i]), row(rg_lambda[i]),
            rg_w_out[i].astype(BF16), tm=256)

    k, v_t, k_mean = _shared_kv(x, kv_mod[0][:, None, :], row(kv_norm_g),
                                w_kv[:, :d_att].astype(BF16), w_kv[:, d_att:].T.astype(BF16))
    k_mean = k_mean.reshape(bsz, n_blk, n_heads, HEAD_DIM).transpose(0, 2, 1, 3)

    for i in range(n_b):
        layer = n_a + i
        x = _moba_layer(
            x, mod[layer][:, None, :], row(norm_g[layer]), att_w_in[i].astype(BF16), k, v_t,
            k_mean, att_w_out[i].astype(BF16), row(final_norm_g),
            apply_final_norm=(i == n_b - 1))
    return x
```

```python
import functools

import jax
import jax.numpy as jnp
from jax import lax
from jax.experimental import pallas as pl
from jax.experimental.pallas import tpu as pltpu

EPS = 1e-6
RG_C = 8.0
HEAD_DIM = 128
MOBA_BLOCK = 256
MOBA_TOPK = 3
NEG_INF = -1e30
LOG2E = 1.4426950408889634
SUBLANES = 8
VMEM_LIMIT_BYTES = 56 * 1024 * 1024

F32 = jnp.float32
BF16 = jnp.bfloat16
NT_DIMS = (((1,), (1,)), ((), ()))


def _sigmoid(z):
    return 1.0 / (1.0 + jnp.exp(-z))


def _norm_modulate(x, norm_g, shift, scale):
    ms = jnp.mean(x * x, axis=-1, keepdims=True)
    return x * lax.rsqrt(ms + EPS) * (norm_g * (1.0 + scale)) + shift


def _mod_kernel(c_ref, w_ref, b_ref, o_ref):
    c = c_ref[...]
    cs = (c * _sigmoid(c)).astype(BF16)
    w = w_ref[0].astype(BF16)
    o_ref[0] = jnp.dot(cs, w, preferred_element_type=F32) + b_ref[0]


def _modulation(c, w, b, tn):
    n_layers, d, n = w.shape
    bsz = c.shape[0]
    return pl.pallas_call(
        _mod_kernel,
        grid=(n_layers, n // tn),
        in_specs=[
            pl.BlockSpec((bsz, d), lambda l, j: (0, 0)),
            pl.BlockSpec((1, d, tn), lambda l, j: (l, 0, j)),
            pl.BlockSpec((1, 1, tn), lambda l, j: (l, 0, j)),
        ],
        out_specs=pl.BlockSpec((1, bsz, tn), lambda l, j: (l, 0, j)),
        out_shape=jax.ShapeDtypeStruct((n_layers, bsz, n), F32),
        compiler_params=pltpu.CompilerParams(
            dimension_semantics=("arbitrary", "arbitrary"),
            vmem_limit_bytes=VMEM_LIMIT_BYTES),
        name="adaln_mod",
    )(c, w, b.reshape(n_layers, 1, n))


def _rglru_kernel(x_ref, mod_ref, ng_ref, win_ref, cw_ref, cb_ref, wax_ref, ba_ref,
                  bx_ref, lam_ref, wout_ref, o_ref,
                  ug_s, ubuf, a_s, b_s, hstate):
    tm = x_ref.shape[1]
    d = x_ref.shape[2]
    n_heads, rb, _ = wax_ref.shape
    conv_w = cw_ref.shape[0]

    @pl.when(pl.program_id(1) == 0)
    def _():
        ubuf[0:SUBLANES, :] = jnp.zeros((SUBLANES, d), F32)
        hstate[...] = jnp.zeros_like(hstate)

    x = x_ref[0]
    mod = mod_ref[0]
    shift, scale, gate = mod[:, :d], mod[:, d:2 * d], mod[:, 2 * d:]
    h = _norm_modulate(x, ng_ref[...], shift, scale)
    ug_s[...] = jnp.dot(h.astype(BF16), win_ref[...], preferred_element_type=F32)

    ubuf[SUBLANES:SUBLANES + tm, :] = ug_s[:, :d]
    uc = cb_ref[...] + cw_ref[conv_w - 1:conv_w, :] * ug_s[:, :d]
    for k in range(conv_w - 1):
        back = conv_w - 1 - k
        uc = uc + cw_ref[k:k + 1, :] * ubuf[SUBLANES - back:SUBLANES - back + tm, :]
    ubuf[0:SUBLANES, :] = ubuf[tm:tm + SUBLANES, :]

    lam = lam_ref[...]
    softplus_neg_lam = jnp.maximum(-lam, 0.0) + jnp.log1p(jnp.exp(-jnp.abs(lam)))
    for hh in range(n_heads):
        sl = slice(hh * rb, (hh + 1) * rb)
        uch = uc[:, sl]
        z = jnp.dot(uch.astype(BF16), wax_ref[hh], preferred_element_type=F32)
        r = _sigmoid(z[:, :rb] + ba_ref[:, sl])
        gi = _sigmoid(z[:, rb:] + bx_ref[:, sl])
        log_a = (-RG_C) * r * softplus_neg_lam[:, sl]
        a = jnp.exp(log_a)
        a_s[:, sl] = a
        b_s[:, sl] = jnp.sqrt(1.0 - a * a) * (gi * uch)

    row = lax.broadcasted_iota(jnp.int32, (SUBLANES, d), 0)

    def scan_group(g, h_prev):
        r0 = pl.multiple_of(g * SUBLANES, SUBLANES)
        a = a_s[pl.ds(r0, SUBLANES), :]
        b = b_s[pl.ds(r0, SUBLANES), :]
        step = 1
        while step < SUBLANES:
            keep = row >= step
            a_sh = pltpu.roll(a, step, 0)
            b_sh = pltpu.roll(b, step, 0)
            b = jnp.where(keep, a * b_sh + b, b)
            a = jnp.where(keep, a * a_sh, a)
            step *= 2
        hs = a * h_prev + b
        b_s[pl.ds(r0, SUBLANES), :] = hs
        return hs[SUBLANES - 1:SUBLANES, :]

    hstate[...] = lax.fori_loop(0, tm // SUBLANES, scan_group, hstate[...])

    gpath = ug_s[:, d:]
    y = b_s[...] * (gpath * _sigmoid(gpath))
    o_ref[0] = x + gate * jnp.dot(y.astype(BF16), wout_ref[...], preferred_element_type=F32)


def _rglru_layer(x, mod, norm_g, w_in, conv_w, conv_b, w_ax, b_a, b_x, lam, w_out, tm):
    bsz, seq, d = x.shape
    n_heads, rb, _ = w_ax.shape
    full = lambda shape: pl.BlockSpec(shape, lambda b, i: (0,) * len(shape))
    return pl.pallas_call(
        _rglru_kernel,
        grid=(bsz, seq // tm),
        in_specs=[
            pl.BlockSpec((1, tm, d), lambda b, i: (b, i, 0)),
            pl.BlockSpec((1, 1, 3 * d), lambda b, i: (b, 0, 0)),
            full((1, d)),
            full((d, 2 * d)),
            full(conv_w.shape),
            full((1, d)),
            full((n_heads, rb, 2 * rb)),
            full((1, d)),
            full((1, d)),
            full((1, d)),
            full((d, d)),
        ],
        out_specs=pl.BlockSpec((1, tm, d), lambda b, i: (b, i, 0)),
        out_shape=jax.ShapeDtypeStruct(x.shape, F32),
        scratch_shapes=[
            pltpu.VMEM((tm, 2 * d), F32),
            pltpu.VMEM((tm + SUBLANES, d), F32),
            pltpu.VMEM((tm, d), F32),
            pltpu.VMEM((tm, d), F32),
            pltpu.VMEM((1, d), F32),
        ],
        compiler_params=pltpu.CompilerParams(
            dimension_semantics=("arbitrary", "arbitrary"),
            vmem_limit_bytes=VMEM_LIMIT_BYTES),
        name="rglru_layer",
    )(x, mod, norm_g, w_in, conv_w, conv_b, w_ax, b_a, b_x, lam, w_out)


def _kv_kernel(x_ref, mod_ref, ng_ref, wk_ref, wvt_ref, k_ref, vt_ref, km_ref):
    d = x_ref.shape[2]
    n_heads = k_ref.shape[1]
    mod = mod_ref[0]
    h = _norm_modulate(x_ref[0], ng_ref[...], mod[:, :d], mod[:, d:]).astype(BF16)
    k = jnp.dot(h, wk_ref[...], preferred_element_type=F32)
    vt = lax.dot_general(wvt_ref[...], h, NT_DIMS, preferred_element_type=F32)
    km_ref[0, 0] = jnp.mean(k, axis=0, keepdims=True)
    for hd in range(n_heads):
        sl = slice(hd * HEAD_DIM, (hd + 1) * HEAD_DIM)
        k_ref[0, hd] = k[:, sl].astype(BF16)
        vt_ref[0, hd] = vt[sl, :].astype(BF16)


def _shared_kv(x, mod, norm_g, w_k, w_v_t):
    bsz, seq, d = x.shape
    d_att = w_k.shape[1]
    n_heads = d_att // HEAD_DIM
    n_blk = seq // MOBA_BLOCK
    return pl.pallas_call(
        _kv_kernel,
        grid=(bsz, n_blk),
        in_specs=[
            pl.BlockSpec((1, MOBA_BLOCK, d), lambda b, i: (b, i, 0)),
            pl.BlockSpec((1, 1, 2 * d), lambda b, i: (b, 0, 0)),
            pl.BlockSpec((1, d), lambda b, i: (0, 0)),
            pl.BlockSpec((d, d_att), lambda b, i: (0, 0)),
            pl.BlockSpec((d_att, d), lambda b, i: (0, 0)),
        ],
        out_specs=[
            pl.BlockSpec((1, n_heads, MOBA_BLOCK, HEAD_DIM), lambda b, i: (b, 0, i, 0)),
            pl.BlockSpec((1, n_heads, HEAD_DIM, MOBA_BLOCK), lambda b, i: (b, 0, 0, i)),
            pl.BlockSpec((1, 1, 1, d_att), lambda b, i: (b, i, 0, 0)),
        ],
        out_shape=[jax.ShapeDtypeStruct((bsz, n_heads, seq, HEAD_DIM), BF16),
                   jax.ShapeDtypeStruct((bsz, n_heads, HEAD_DIM, seq), BF16),
                   jax.ShapeDtypeStruct((bsz, n_blk, 1, d_att), F32)],
        compiler_params=pltpu.CompilerParams(
            dimension_semantics=("arbitrary", "arbitrary"),
            vmem_limit_bytes=VMEM_LIMIT_BYTES),
        name="shared_kv",
    )(x, mod, norm_g, w_k, w_v_t)


def _select_blocks(gate_t, n_past, sel_s):
    blk = lax.broadcasted_iota(jnp.int32, gate_t.shape, 0)
    g = jnp.where(blk < n_past, gate_t, NEG_INF)
    sel_s[...] = g
    rank = jnp.zeros(gate_t.shape, jnp.int32)
    for j in range(n_past):
        gj = jnp.broadcast_to(sel_s[j:j + 1, :], gate_t.shape)
        ahead = (gj > g) | ((gj == g) & (blk > j))
        rank = rank + ahead.astype(jnp.int32)
    sel_s[...] = jnp.where((rank < MOBA_TOPK) & (blk < n_past), 0.0, NEG_INF)


def _fold_rows(a, op):
    return op(a.reshape(a.shape[0] // SUBLANES, SUBLANES, a.shape[1]), axis=0)


def _head_scores(hd, slot, n_past, q_s, k_ref, km_ref, sel_s, sc_s, m_s):
    qf = q_s[hd]
    qh = (qf * (HEAD_DIM ** -0.5 * LOG2E)).astype(BF16)
    masked = n_past > MOBA_TOPK
    if masked:
        gate_t = lax.dot_general(km_ref[0, hd].astype(BF16), qf.astype(BF16), NT_DIMS,
                                 preferred_element_type=F32)
        _select_blocks(gate_t, n_past, sel_s)
    m8 = None
    for j in range(n_past + 1):
        k_j = k_ref[0, hd, j * MOBA_BLOCK:(j + 1) * MOBA_BLOCK, :]
        s = lax.dot_general(k_j, qh, NT_DIMS, preferred_element_type=F32)
        if j == n_past:
            key = lax.broadcasted_iota(jnp.int32, s.shape, 0)
            qry = lax.broadcasted_iota(jnp.int32, s.shape, 1)
            s = jnp.where(key <= qry, s, NEG_INF)
        elif masked:
            s = s + sel_s[j:j + 1, :]
        sc_s[slot, j] = s
        smax = _fold_rows(s, jnp.max)
        m8 = smax if m8 is None else jnp.maximum(m8, smax)
    m_s[slot] = m8


def _head_output(hd, slot, n_past, vt_ref, sc_s, m_s, o_s):
    m = jnp.max(m_s[slot], axis=0, keepdims=True)
    l8 = None
    o_t = None
    for j in range(n_past + 1):
        p = jnp.exp2(sc_s[slot, j] - m)
        psum = _fold_rows(p, jnp.sum)
        l8 = psum if l8 is None else l8 + psum
        pv = jnp.dot(vt_ref[0, hd, :, j * MOBA_BLOCK:(j + 1) * MOBA_BLOCK], p.astype(BF16),
                     preferred_element_type=F32)
        o_t = pv if o_t is None else o_t + pv
    o_t = o_t * (1.0 / jnp.sum(l8, axis=0, keepdims=True))
    o_s[hd] = o_t.T


def _attend(n_past, n_heads, q_s, k_ref, vt_ref, km_ref, o_s, sel_s, sc_s, m_s):
    scores = functools.partial(_head_scores, n_past=n_past, q_s=q_s, k_ref=k_ref, km_ref=km_ref,
                               sel_s=sel_s, sc_s=sc_s, m_s=m_s)
    output = functools.partial(_head_output, n_past=n_past, vt_ref=vt_ref, sc_s=sc_s, m_s=m_s,
                               o_s=o_s)
    scores(0, 0)

    def step(hd, carry):
        output(hd - 1, (hd - 1) & 1)
        scores(hd, hd & 1)
        return carry

    lax.fori_loop(1, n_heads, step, 0)
    output(n_heads - 1, (n_heads - 1) & 1)


def _moba_kernel(x_ref, mod_ref, ng_ref, win_ref, k_ref, vt_ref, km_ref, wout_ref, fg_ref,
                 o_ref, q_s, g_s, o_s, sel_s, sc_s, m_s, *, apply_final_norm):
    d = x_ref.shape[2]
    n_heads = k_ref.shape[1]
    n_blk = km_ref.shape[2]
    d_att = n_heads * HEAD_DIM
    qb = pl.program_id(1)

    x = x_ref[0]
    mod = mod_ref[0]
    shift, scale, gate = mod[:, :d], mod[:, d:2 * d], mod[:, 2 * d:]
    h = _norm_modulate(x, ng_ref[...], shift, scale)
    qg = jnp.dot(h.astype(BF16), win_ref[...], preferred_element_type=F32)
    for hd in range(n_heads):
        q_s[hd] = qg[:, hd * HEAD_DIM:(hd + 1) * HEAD_DIM]
    g_s[...] = qg[:, d_att:]

    for n_past in range(n_blk):
        @pl.when(qb == n_past)
        def _(n_past=n_past):
            _attend(n_past, n_heads, q_s, k_ref, vt_ref, km_ref, o_s, sel_s, sc_s, m_s)

    for hd in range(n_heads):
        sl = slice(hd * HEAD_DIM, (hd + 1) * HEAD_DIM)
        gp = g_s[:, sl]
        g_s[:, sl] = o_s[hd] * (gp * _sigmoid(gp))
    out = x + gate * jnp.dot(g_s[...].astype(BF16), wout_ref[...], preferred_element_type=F32)
    if apply_final_norm:
        ms = jnp.mean(out * out, axis=-1, keepdims=True)
        out = out * lax.rsqrt(ms + EPS) * fg_ref[...]
    o_ref[0] = out


def _moba_layer(x, mod, norm_g, w_in, k, v_t, k_mean, w_out, final_g, apply_final_norm):
    bsz, seq, d = x.shape
    n_heads = k.shape[1]
    d_att = n_heads * HEAD_DIM
    n_blk = seq // MOBA_BLOCK
    full = lambda shape: pl.BlockSpec(shape, lambda b, i: (0,) * len(shape))
    per_batch = lambda shape: pl.BlockSpec(shape, lambda b, i: (b,) + (0,) * (len(shape) - 1))
    return pl.pallas_call(
        functools.partial(_moba_kernel, apply_final_norm=apply_final_norm),
        grid=(bsz, n_blk),
        in_specs=[
            pl.BlockSpec((1, MOBA_BLOCK, d), lambda b, i: (b, i, 0)),
            per_batch((1, 1, 3 * d)),
            full((1, d)),
            full((d, 2 * d_att)),
            per_batch((1, n_heads, seq, HEAD_DIM)),
            per_batch((1, n_heads, HEAD_DIM, seq)),
            per_batch((1, n_heads, n_blk, HEAD_DIM)),
            full((d_att, d)),
            full((1, d)),
        ],
        out_specs=pl.BlockSpec((1, MOBA_BLOCK, d), lambda b, i: (b, i, 0)),
        out_shape=jax.ShapeDtypeStruct(x.shape, F32),
        scratch_shapes=[
            pltpu.VMEM((n_heads, MOBA_BLOCK, HEAD_DIM), F32),
            pltpu.VMEM((MOBA_BLOCK, d_att), F32),
            pltpu.VMEM((n_heads, MOBA_BLOCK, HEAD_DIM), F32),
            pltpu.VMEM((n_blk, MOBA_BLOCK), F32),
            pltpu.VMEM((2, n_blk, MOBA_BLOCK, MOBA_BLOCK), F32),
            pltpu.VMEM((2, SUBLANES, MOBA_BLOCK), F32),
        ],
        compiler_params=pltpu.CompilerParams(
            dimension_semantics=("arbitrary", "arbitrary"),
            vmem_limit_bytes=VMEM_LIMIT_BYTES),
        name="moba_layer",
    )(x, mod, norm_g, w_in, k, v_t, k_mean, w_out, final_g)


def kernel(x, c, mod_w, mod_b, norm_g, rg_w_in, rg_conv_w, rg_conv_b, rg_w_a, rg_b_a, rg_w_x,
           rg_b_x, rg_lambda, rg_w_out, kv_norm_g, kv_mod_w, kv_mod_b, w_kv, att_w_in,
           att_w_out, final_norm_g):
    bsz, seq, d = x.shape
    depth = mod_w.shape[0]
    n_a = rg_w_in.shape[0]
    n_b = att_w_in.shape[0]
    assert depth == n_a + n_b and seq % MOBA_BLOCK == 0
    d_att = w_kv.shape[1] // 2
    n_heads = d_att // HEAD_DIM
    n_blk = seq // MOBA_BLOCK

    mod = _modulation(c, mod_w, mod_b, tn=768)
    kv_mod = _modulation(c, kv_mod_w[None], kv_mod_b[None], tn=1024)

    row = lambda p: p.reshape(1, -1)
    for i in range(n_a):
        w_ax = jnp.concatenate([rg_w_a[i], rg_w_x[i]], axis=-1).astype(BF16)
        x = _rglru_layer(
            x, mod[i][:, None, :], row(norm_g[i]), rg_w_in[i].astype(BF16), rg_conv_w[i],
            row(rg_conv_b[i]), w_ax, row(rg_b_a[i]), row(rg_b_x[i]), row(rg_lambda[i]),
            rg_w_out[i].astype(BF16), tm=256)

    k, v_t, k_mean = _shared_kv(x, kv_mod[0][:, None, :], row(kv_norm_g),
                                w_kv[:, :d_att].astype(BF16), w_kv[:, d_att:].T.astype(BF16))
    k_mean = k_mean.reshape(bsz, n_blk, n_heads, HEAD_DIM).transpose(0, 2, 1, 3)

    for i in range(n_b):
        layer = n_a + i
        x = _moba_layer(
            x, mod[layer][:, None, :], row(norm_g[layer]), att_w_in[i].astype(BF16), k, v_t,
            k_mean, att_w_out[i].astype(BF16), row(final_norm_g),
            apply_final_norm=(i == n_b - 1))
    return x
```

```python
import functools

import jax
import jax.numpy as jnp
from jax import lax
from jax.experimental import pallas as pl
from jax.experimental.pallas import tpu as pltpu

EPS = 1e-6
RG_C = 8.0
HEAD_DIM = 128
MOBA_BLOCK = 256
MOBA_TOPK = 3
NEG_INF = -1e30
LOG2E = 1.4426950408889634
SUBLANES = 8
VMEM_LIMIT_BYTES = 56 * 1024 * 1024

F32 = jnp.float32
BF16 = jnp.bfloat16
NT_DIMS = (((1,), (1,)), ((), ()))


def _sigmoid(z):
    return 1.0 / (1.0 + jnp.exp2(z * (-LOG2E)))


def _sqrt_nonneg(v):
    return jnp.where(v > 0.0, v * lax.rsqrt(v), 0.0)


def _norm_modulate(x, norm_g, shift, scale):
    ms = jnp.mean(x * x, axis=-1, keepdims=True)
    return x * lax.rsqrt(ms + EPS) * (norm_g * (1.0 + scale)) + shift


def _mod_kernel(c_ref, w_ref, b_ref, o_ref):
    c = c_ref[...]
    cs = (c * _sigmoid(c)).astype(BF16)
    w = w_ref[0].astype(BF16)
    o_ref[0] = jnp.dot(cs, w, preferred_element_type=F32) + b_ref[0]


def _modulation(c, w, b, tn):
    n_layers, d, n = w.shape
    bsz = c.shape[0]
    return pl.pallas_call(
        _mod_kernel,
        grid=(n_layers, n // tn),
        in_specs=[
            pl.BlockSpec((bsz, d), lambda l, j: (0, 0)),
            pl.BlockSpec((1, d, tn), lambda l, j: (l, 0, j)),
            pl.BlockSpec((1, 1, tn), lambda l, j: (l, 0, j)),
        ],
        out_specs=pl.BlockSpec((1, bsz, tn), lambda l, j: (l, 0, j)),
        out_shape=jax.ShapeDtypeStruct((n_layers, bsz, n), F32),
        compiler_params=pltpu.CompilerParams(
            dimension_semantics=("arbitrary", "arbitrary"),
            vmem_limit_bytes=VMEM_LIMIT_BYTES),
        name="adaln_mod",
    )(c, w, b.reshape(n_layers, 1, n))


def _tile_copies(hbm_ref, buf, sem, tile, slot, to_hbm):
    t_len, bsz, _ = buf.shape[1:]
    copies = []
    for b in range(bsz):
        hbm = hbm_ref.at[b, pl.ds(tile * t_len, t_len), :]
        vmem = buf.at[slot, :, b, :]
        src, dst = (vmem, hbm) if to_hbm else (hbm, vmem)
        copies.append(pltpu.make_async_copy(src, dst, sem.at[slot]))
    return copies


def _rglru_kernel(x_hbm, mod_ref, ng_ref, win_ref, cw_ref, cb_ref, wax_ref, ba_ref,
                  bx_ref, lam_ref, wout_ref, o_hbm,
                  xbuf, obuf, in_sem, out_sem, hs_s, y_s, utail, hstate):
    _, t_len, bsz, d = xbuf.shape
    rows = t_len * bsz
    n_heads, rb, _ = wax_ref.shape
    conv_w = cw_ref.shape[0]
    i = pl.program_id(0)
    n_tiles = x_hbm.shape[1] // t_len
    slot = i & 1

    @pl.when(i == 0)
    def _():
        for c in _tile_copies(x_hbm, xbuf, in_sem, 0, 0, to_hbm=False):
            c.start()
        utail[...] = jnp.zeros_like(utail)
        hstate[...] = jnp.zeros_like(hstate)

    @pl.when(i + 1 < n_tiles)
    def _():
        for c in _tile_copies(x_hbm, xbuf, in_sem, i + 1, 1 - slot, to_hbm=False):
            c.start()

    for c in _tile_copies(x_hbm, xbuf, in_sem, i, slot, to_hbm=False):
        c.wait()

    x3 = xbuf[slot]
    mod = mod_ref[...]
    shift, scale, gate = mod[:, :d], mod[:, d:2 * d], mod[:, 2 * d:]
    ms = jnp.mean(x3 * x3, axis=-1, keepdims=True)
    h3 = x3 * lax.rsqrt(ms + EPS) * (ng_ref[...] * (1.0 + scale)) + shift
    hb = h3.reshape(rows, d).astype(BF16)
    u3 = jnp.dot(hb, win_ref[:, :d], preferred_element_type=F32).reshape(t_len, bsz, d)

    upad = jnp.concatenate([utail[...], u3], axis=0)
    utail[...] = u3[t_len - (conv_w - 1):]
    uc3 = cb_ref[...] + cw_ref[conv_w - 1:conv_w, :] * u3
    for k in range(conv_w - 1):
        uc3 = uc3 + cw_ref[k:k + 1, :] * upad[k:k + t_len]
    uc = uc3.reshape(rows, d)

    lam = lam_ref[...]
    softplus_neg_lam = jnp.maximum(-lam, 0.0) + jnp.log1p(jnp.exp(-jnp.abs(lam)))
    log2_a_per_r = (-RG_C * LOG2E) * softplus_neg_lam
    for hh in range(n_heads):
        sl = slice(hh * rb, (hh + 1) * rb)
        uch = uc[:, sl]
        z = jnp.dot(uch.astype(BF16), wax_ref[hh], preferred_element_type=F32)
        r = _sigmoid(z[:, :rb] + ba_ref[:, sl])
        gi = _sigmoid(z[:, rb:] + bx_ref[:, sl])
        a = jnp.exp2(r * log2_a_per_r[:, sl])
        b_in = _sqrt_nonneg(1.0 - a * a) * (gi * uch)
        a3 = a.reshape(t_len, bsz, rb)
        b3 = b_in.reshape(t_len, bsz, rb)
        h_run = hstate[:, sl]
        for t in range(t_len):
            h_run = a3[t] * h_run + b3[t]
            hs_s[t, :, sl] = h_run
        hstate[:, sl] = h_run
        gpath = jnp.dot(hb, win_ref[:, d + hh * rb:d + (hh + 1) * rb], preferred_element_type=F32)
        y = hs_s[:, :, sl].reshape(rows, rb) * (gpath * _sigmoid(gpath))
        y_s[:, sl] = y.astype(BF16)

    proj = jnp.dot(y_s[...], wout_ref[...], preferred_element_type=F32)
    out3 = x3 + gate * proj.reshape(t_len, bsz, d)

    @pl.when(i >= 2)
    def _():
        for c in _tile_copies(o_hbm, obuf, out_sem, i - 2, slot, to_hbm=True):
            c.wait()

    obuf[slot] = out3
    for c in _tile_copies(o_hbm, obuf, out_sem, i, slot, to_hbm=True):
        c.start()

    @pl.when(i == n_tiles - 1)
    def _():
        if n_tiles >= 2:
            for c in _tile_copies(o_hbm, obuf, out_sem, i - 1, 1 - slot, to_hbm=True):
                c.wait()
        for c in _tile_copies(o_hbm, obuf, out_sem, i, slot, to_hbm=True):
            c.wait()


def _rglru_layer(x, mod, norm_g, w_in, conv_w, conv_b, w_ax, b_a, b_x, lam, w_out, t_len):
    bsz, seq, d = x.shape
    assert bsz == SUBLANES and seq % t_len == 0 and t_len >= conv_w.shape[0] - 1
    n_heads, rb, _ = w_ax.shape
    full = lambda shape: pl.BlockSpec(shape, lambda i: (0,) * len(shape))
    return pl.pallas_call(
        _rglru_kernel,
        grid=(seq // t_len,),
        in_specs=[
            pl.BlockSpec(memory_space=pl.ANY),
            full((bsz, 3 * d)),
            full((1, d)),
            full((d, 2 * d)),
            full(conv_w.shape),
            full((1, d)),
            full((n_heads, rb, 2 * rb)),
            full((1, d)),
            full((1, d)),
            full((1, d)),
            full((d, d)),
        ],
        out_specs=pl.BlockSpec(memory_space=pl.ANY),
        out_shape=jax.ShapeDtypeStruct(x.shape, F32),
        scratch_shapes=[
            pltpu.VMEM((2, t_len, bsz, d), F32),
            pltpu.VMEM((2, t_len, bsz, d), F32),
            pltpu.SemaphoreType.DMA((2,)),
            pltpu.SemaphoreType.DMA((2,)),
            pltpu.VMEM((t_len, bsz, d), F32),
            pltpu.VMEM((t_len * bsz, d), BF16),
            pltpu.VMEM((conv_w.shape[0] - 1, bsz, d), F32),
            pltpu.VMEM((bsz, d), F32),
        ],
        compiler_params=pltpu.CompilerParams(
            dimension_semantics=("arbitrary",),
            vmem_limit_bytes=VMEM_LIMIT_BYTES),
        name="rglru_layer",
    )(x, mod, norm_g, w_in, conv_w, conv_b, w_ax, b_a, b_x, lam, w_out)


def _kv_kernel(x_ref, mod_ref, ng_ref, wk_ref, wvt_ref, k_ref, vt_ref, km_ref):
    d = x_ref.shape[2]
    n_heads = k_ref.shape[1]
    mod = mod_ref[0]
    h = _norm_modulate(x_ref[0], ng_ref[...], mod[:, :d], mod[:, d:]).astype(BF16)
    k = jnp.dot(h, wk_ref[...], preferred_element_type=F32)
    vt = lax.dot_general(wvt_ref[...], h, NT_DIMS, preferred_element_type=F32)
    km_ref[0, 0] = jnp.mean(k, axis=0, keepdims=True)
    for hd in range(n_heads):
        sl = slice(hd * HEAD_DIM, (hd + 1) * HEAD_DIM)
        k_ref[0, hd] = k[:, sl].astype(BF16)
        vt_ref[0, hd] = vt[sl, :].astype(BF16)


def _shared_kv(x, mod, norm_g, w_k, w_v_t):
    bsz, seq, d = x.shape
    d_att = w_k.shape[1]
    n_heads = d_att // HEAD_DIM
    n_blk = seq // MOBA_BLOCK
    return pl.pallas_call(
        _kv_kernel,
        grid=(bsz, n_blk),
        in_specs=[
            pl.BlockSpec((1, MOBA_BLOCK, d), lambda b, i: (b, i, 0)),
            pl.BlockSpec((1, 1, 2 * d), lambda b, i: (b, 0, 0)),
            pl.BlockSpec((1, d), lambda b, i: (0, 0)),
            pl.BlockSpec((d, d_att), lambda b, i: (0, 0)),
            pl.BlockSpec((d_att, d), lambda b, i: (0, 0)),
        ],
        out_specs=[
            pl.BlockSpec((1, n_heads, MOBA_BLOCK, HEAD_DIM), lambda b, i: (b, 0, i, 0)),
            pl.BlockSpec((1, n_heads, HEAD_DIM, MOBA_BLOCK), lambda b, i: (b, 0, 0, i)),
            pl.BlockSpec((1, 1, 1, d_att), lambda b, i: (b, i, 0, 0)),
        ],
        out_shape=[jax.ShapeDtypeStruct((bsz, n_heads, seq, HEAD_DIM), BF16),
                   jax.ShapeDtypeStruct((bsz, n_heads, HEAD_DIM, seq), BF16),
                   jax.ShapeDtypeStruct((bsz, n_blk, 1, d_att), F32)],
        compiler_params=pltpu.CompilerParams(
            dimension_semantics=("arbitrary", "arbitrary"),
            vmem_limit_bytes=VMEM_LIMIT_BYTES),
        name="shared_kv",
    )(x, mod, norm_g, w_k, w_v_t)


def _select_blocks(gate_t, n_past, sel_s):
    blk = lax.broadcasted_iota(jnp.int32, gate_t.shape, 0)
    g = jnp.where(blk < n_past, gate_t, NEG_INF)
    sel_s[...] = g
    rank = jnp.zeros(gate_t.shape, jnp.int32)
    for j in range(n_past):
        gj = jnp.broadcast_to(sel_s[j:j + 1, :], gate_t.shape)
        ahead = (gj > g) | ((gj == g) & (blk > j))
        rank = rank + ahead.astype(jnp.int32)
    sel_s[...] = jnp.where((rank < MOBA_TOPK) & (blk < n_past), 0.0, NEG_INF)


def _fold_rows(a, op):
    return op(a.reshape(a.shape[0] // SUBLANES, SUBLANES, a.shape[1]), axis=0)


def _head_scores(hd, slot, n_past, q_s, k_ref, km_ref, sel_s, sc_s, m_s):
    qf = q_s[hd]
    qh = (qf * (HEAD_DIM ** -0.5 * LOG2E)).astype(BF16)
    masked = n_past > MOBA_TOPK
    if masked:
        gate_t = lax.dot_general(km_ref[0, hd].astype(BF16), qf.astype(BF16), NT_DIMS,
                                 preferred_element_type=F32)
        _select_blocks(gate_t, n_past, sel_s)
    m8 = None
    for j in range(n_past + 1):
        k_j = k_ref[0, hd, j * MOBA_BLOCK:(j + 1) * MOBA_BLOCK, :]
        s = lax.dot_general(k_j, qh, NT_DIMS, preferred_element_type=F32)
        if j == n_past:
            key = lax.broadcasted_iota(jnp.int32, s.shape, 0)
            qry = lax.broadcasted_iota(jnp.int32, s.shape, 1)
            s = jnp.where(key <= qry, s, NEG_INF)
        elif masked:
            s = s + sel_s[j:j + 1, :]
        sc_s[slot, j] = s
        smax = _fold_rows(s, jnp.max)
        m8 = smax if m8 is None else jnp.maximum(m8, smax)
    m_s[slot] = m8


def _head_output(hd, slot, n_past, vt_ref, sc_s, m_s, o_s):
    m = jnp.max(m_s[slot], axis=0, keepdims=True)
    l8 = None
    o_t = None
    for j in range(n_past + 1):
        p = jnp.exp2(sc_s[slot, j] - m)
        psum = _fold_rows(p, jnp.sum)
        l8 = psum if l8 is None else l8 + psum
        pv = jnp.dot(vt_ref[0, hd, :, j * MOBA_BLOCK:(j + 1) * MOBA_BLOCK], p.astype(BF16),
                     preferred_element_type=F32)
        o_t = pv if o_t is None else o_t + pv
    o_t = o_t * (1.0 / jnp.sum(l8, axis=0, keepdims=True))
    o_s[hd] = o_t.T


def _attend(n_past, n_heads, q_s, k_ref, vt_ref, km_ref, o_s, sel_s, sc_s, m_s):
    scores = functools.partial(_head_scores, n_past=n_past, q_s=q_s, k_ref=k_ref, km_ref=km_ref,
                               sel_s=sel_s, sc_s=sc_s, m_s=m_s)
    output = functools.partial(_head_output, n_past=n_past, vt_ref=vt_ref, sc_s=sc_s, m_s=m_s,
                               o_s=o_s)
    scores(0, 0)

    def step(hd, carry):
        output(hd - 1, (hd - 1) & 1)
        scores(hd, hd & 1)
        return carry

    lax.fori_loop(1, n_heads, step, 0)
    output(n_heads - 1, (n_heads - 1) & 1)


def _moba_kernel(x_ref, mod_ref, ng_ref, win_ref, k_ref, vt_ref, km_ref, wout_ref, fg_ref,
                 o_ref, q_s, g_s, o_s, sel_s, sc_s, m_s, *, apply_final_norm):
    d = x_ref.shape[2]
    n_heads = k_ref.shape[1]
    n_blk = km_ref.shape[2]
    d_att = n_heads * HEAD_DIM
    qb = pl.program_id(1)

    x = x_ref[0]
    mod = mod_ref[0]
    shift, scale, gate = mod[:, :d], mod[:, d:2 * d], mod[:, 2 * d:]
    h = _norm_modulate(x, ng_ref[...], shift, scale)
    qg = jnp.dot(h.astype(BF16), win_ref[...], preferred_element_type=F32)
    for hd in range(n_heads):
        q_s[hd] = qg[:, hd * HEAD_DIM:(hd + 1) * HEAD_DIM]
    g_s[...] = qg[:, d_att:]

    for n_past in range(n_blk):
        @pl.when(qb == n_past)
        def _(n_past=n_past):
            _attend(n_past, n_heads, q_s, k_ref, vt_ref, km_ref, o_s, sel_s, sc_s, m_s)

    for hd in range(n_heads):
        sl = slice(hd * HEAD_DIM, (hd + 1) * HEAD_DIM)
        gp = g_s[:, sl]
        g_s[:, sl] = o_s[hd] * (gp * _sigmoid(gp))
    out = x + gate * jnp.dot(g_s[...].astype(BF16), wout_ref[...], preferred_element_type=F32)
    if apply_final_norm:
        ms = jnp.mean(out * out, axis=-1, keepdims=True)
        out = out * lax.rsqrt(ms + EPS) * fg_ref[...]
    o_ref[0] = out


def _moba_layer(x, mod, norm_g, w_in, k, v_t, k_mean, w_out, final_g, apply_final_norm):
    bsz, seq, d = x.shape
    n_heads = k.shape[1]
    d_att = n_heads * HEAD_DIM
    n_blk = seq // MOBA_BLOCK
    full = lambda shape: pl.BlockSpec(shape, lambda b, i: (0,) * len(shape))
    per_batch = lambda shape: pl.BlockSpec(shape, lambda b, i: (b,) + (0,) * (len(shape) - 1))
    return pl.pallas_call(
        functools.partial(_moba_kernel, apply_final_norm=apply_final_norm),
        grid=(bsz, n_blk),
        in_specs=[
            pl.BlockSpec((1, MOBA_BLOCK, d), lambda b, i: (b, i, 0)),
            per_batch((1, 1, 3 * d)),
            full((1, d)),
            full((d, 2 * d_att)),
            per_batch((1, n_heads, seq, HEAD_DIM)),
            per_batch((1, n_heads, HEAD_DIM, seq)),
            per_batch((1, n_heads, n_blk, HEAD_DIM)),
            full((d_att, d)),
            full((1, d)),
        ],
        out_specs=pl.BlockSpec((1, MOBA_BLOCK, d), lambda b, i: (b, i, 0)),
        out_shape=jax.ShapeDtypeStruct(x.shape, F32),
        scratch_shapes=[
            pltpu.VMEM((n_heads, MOBA_BLOCK, HEAD_DIM), F32),
            pltpu.VMEM((MOBA_BLOCK, d_att), F32),
            pltpu.VMEM((n_heads, MOBA_BLOCK, HEAD_DIM), F32),
            pltpu.VMEM((n_blk, MOBA_BLOCK), F32),
            pltpu.VMEM((2, n_blk, MOBA_BLOCK, MOBA_BLOCK), F32),
            pltpu.VMEM((2, SUBLANES, MOBA_BLOCK), F32),
        ],
        compiler_params=pltpu.CompilerParams(
            dimension_semantics=("arbitrary", "arbitrary"),
            vmem_limit_bytes=VMEM_LIMIT_BYTES),
        name="moba_layer",
    )(x, mod, norm_g, w_in, k, v_t, k_mean, w_out, final_g)


def kernel(x, c, mod_w, mod_b, norm_g, rg_w_in, rg_conv_w, rg_conv_b, rg_w_a, rg_b_a, rg_w_x,
           rg_b_x, rg_lambda, rg_w_out, kv_norm_g, kv_mod_w, kv_mod_b, w_kv, att_w_in,
           att_w_out, final_norm_g):
    bsz, seq, d = x.shape
    depth = mod_w.shape[0]
    n_a = rg_w_in.shape[0]
    n_b = att_w_in.shape[0]
    assert depth == n_a + n_b and seq % MOBA_BLOCK == 0
    d_att = w_kv.shape[1] // 2
    n_heads = d_att // HEAD_DIM
    n_blk = seq // MOBA_BLOCK

    mod = _modulation(c, mod_w, mod_b, tn=768)
    kv_mod = _modulation(c, kv_mod_w[None], kv_mod_b[None], tn=1024)

    row = lambda p: p.reshape(1, -1)
    for i in range(n_a):
        w_ax = jnp.concatenate([rg_w_a[i], rg_w_x[i]], axis=-1).astype(BF16)
        x = _rglru_layer(
            x, mod[i], row(norm_g[i]), rg_w_in[i].astype(BF16), rg_conv_w[i],
            row(rg_conv_b[i]), w_ax, row(rg_b_a[i]), row(rg_b_x[i]), row(rg_lambda[i]),
            rg_w_out[i].astype(BF16), t_len=64)

    k, v_t, k_mean = _shared_kv(x, kv_mod[0][:, None, :], row(kv_norm_g),
                                w_kv[:, :d_att].astype(BF16), w_kv[:, d_att:].T.astype(BF16))
    k_mean = k_mean.reshape(bsz, n_blk, n_heads, HEAD_DIM).transpose(0, 2, 1, 3)

    for i in range(n_b):
        layer = n_a + i
        x = _moba_layer(
            x, mod[layer][:, None, :], row(norm_g[layer]), att_w_in[i].astype(BF16), k, v_t,
            k_mean, att_w_out[i].astype(BF16), row(final_norm_g),
            apply_final_norm=(i == n_b - 1))
    return x
```

```python
import functools

import jax
import jax.numpy as jnp
from jax import lax
from jax.experimental import pallas as pl
from jax.experimental.pallas import tpu as pltpu

EPS = 1e-6
RG_C = 8.0
HEAD_DIM = 128
MOBA_BLOCK = 256
MOBA_TOPK = 3
NEG_INF = -1e30
LOG2E = 1.4426950408889634
SUBLANES = 8
MXU_DIM = 256
VMEM_LIMIT_BYTES = 56 * 1024 * 1024

F32 = jnp.float32
BF16 = jnp.bfloat16
NT_DIMS = (((1,), (1,)), ((), ()))


def _sigmoid(z):
    return 1.0 / (1.0 + jnp.exp2(z * (-LOG2E)))


def _sqrt_nonneg(v):
    return jnp.where(v > 0.0, v * lax.rsqrt(v), 0.0)


def _norm_modulate(x, norm_g, shift, scale):
    ms = jnp.mean(x * x, axis=-1, keepdims=True)
    return x * lax.rsqrt(ms + EPS) * (norm_g * (1.0 + scale)) + shift


def _mod_kernel(c_ref, w_ref, b_ref, o_ref):
    c = c_ref[...]
    cs = (c * _sigmoid(c)).astype(BF16)
    w = w_ref[0].astype(BF16)
    o_ref[0] = jnp.dot(cs, w, preferred_element_type=F32) + b_ref[0]


def _modulation(c, w, b, tn):
    n_layers, d, n = w.shape
    bsz = c.shape[0]
    return pl.pallas_call(
        _mod_kernel,
        grid=(n_layers, n // tn),
        in_specs=[
            pl.BlockSpec((bsz, d), lambda l, j: (0, 0)),
            pl.BlockSpec((1, d, tn), lambda l, j: (l, 0, j)),
            pl.BlockSpec((1, 1, tn), lambda l, j: (l, 0, j)),
        ],
        out_specs=pl.BlockSpec((1, bsz, tn), lambda l, j: (l, 0, j)),
        out_shape=jax.ShapeDtypeStruct((n_layers, bsz, n), F32),
        compiler_params=pltpu.CompilerParams(
            dimension_semantics=("arbitrary", "arbitrary"),
            vmem_limit_bytes=VMEM_LIMIT_BYTES),
        name="adaln_mod",
    )(c, w, b.reshape(n_layers, 1, n))


def _tile_copies(hbm_ref, buf, sem, tile, slot, to_hbm):
    t_len, bsz, _ = buf.shape[1:]
    copies = []
    for b in range(bsz):
        hbm = hbm_ref.at[b, pl.ds(tile * t_len, t_len), :]
        vmem = buf.at[slot, :, b, :]
        src, dst = (vmem, hbm) if to_hbm else (hbm, vmem)
        copies.append(pltpu.make_async_copy(src, dst, sem.at[slot]))
    return copies


def _rglru_kernel(x_hbm, mod_ref, ng_ref, win_ref, cw_ref, cb_ref, wax_ref, ba_ref,
                  bx_ref, lam_ref, wout_ref, o_hbm,
                  xbuf, obuf, in_sem, out_sem, hs_s, y_s, utail, hstate):
    _, t_len, bsz, d = xbuf.shape
    rows = t_len * bsz
    n_heads, rb, _ = wax_ref.shape
    conv_w = cw_ref.shape[0]
    i = pl.program_id(0)
    n_tiles = x_hbm.shape[1] // t_len
    slot = i & 1

    @pl.when(i == 0)
    def _():
        for c in _tile_copies(x_hbm, xbuf, in_sem, 0, 0, to_hbm=False):
            c.start()
        utail[...] = jnp.zeros_like(utail)
        hstate[...] = jnp.zeros_like(hstate)

    @pl.when(i + 1 < n_tiles)
    def _():
        for c in _tile_copies(x_hbm, xbuf, in_sem, i + 1, 1 - slot, to_hbm=False):
            c.start()

    for c in _tile_copies(x_hbm, xbuf, in_sem, i, slot, to_hbm=False):
        c.wait()

    x3 = xbuf[slot]
    mod = mod_ref[...]
    shift, scale, gate = mod[:, :d], mod[:, d:2 * d], mod[:, 2 * d:]
    ms = jnp.mean(x3 * x3, axis=-1, keepdims=True)
    h3 = x3 * lax.rsqrt(ms + EPS) * (ng_ref[...] * (1.0 + scale)) + shift
    hb = h3.reshape(rows, d).astype(BF16)
    u3 = jnp.dot(hb, win_ref[:, :d], preferred_element_type=F32).reshape(t_len, bsz, d)

    upad = jnp.concatenate([utail[...], u3], axis=0)
    utail[...] = u3[t_len - (conv_w - 1):]
    uc3 = cb_ref[...] + cw_ref[conv_w - 1:conv_w, :] * u3
    for k in range(conv_w - 1):
        uc3 = uc3 + cw_ref[k:k + 1, :] * upad[k:k + t_len]
    uc = uc3.reshape(rows, d)

    lam = lam_ref[...]
    softplus_neg_lam = jnp.maximum(-lam, 0.0) + jnp.log1p(jnp.exp(-jnp.abs(lam)))
    log2_a_per_r = (-RG_C * LOG2E) * softplus_neg_lam
    for hh in range(n_heads):
        sl = slice(hh * rb, (hh + 1) * rb)
        uch = uc[:, sl]
        z = jnp.dot(uch.astype(BF16), wax_ref[hh], preferred_element_type=F32)
        r = _sigmoid(z[:, :rb] + ba_ref[:, sl])
        gi = _sigmoid(z[:, rb:] + bx_ref[:, sl])
        a = jnp.exp2(r * log2_a_per_r[:, sl])
        b_in = _sqrt_nonneg(1.0 - a * a) * (gi * uch)
        a3 = a.reshape(t_len, bsz, rb)
        b3 = b_in.reshape(t_len, bsz, rb)
        h_run = hstate[:, sl]
        for t in range(t_len):
            h_run = a3[t] * h_run + b3[t]
            hs_s[t, :, sl] = h_run
        hstate[:, sl] = h_run
        gpath = jnp.dot(hb, win_ref[:, d + hh * rb:d + (hh + 1) * rb], preferred_element_type=F32)
        y = hs_s[:, :, sl].reshape(rows, rb) * (gpath * _sigmoid(gpath))
        y_s[:, sl] = y.astype(BF16)

    proj = jnp.dot(y_s[...], wout_ref[...], preferred_element_type=F32)
    out3 = x3 + gate * proj.reshape(t_len, bsz, d)

    @pl.when(i >= 2)
    def _():
        for c in _tile_copies(o_hbm, obuf, out_sem, i - 2, slot, to_hbm=True):
            c.wait()

    obuf[slot] = out3
    for c in _tile_copies(o_hbm, obuf, out_sem, i, slot, to_hbm=True):
        c.start()

    @pl.when(i == n_tiles - 1)
    def _():
        if n_tiles >= 2:
            for c in _tile_copies(o_hbm, obuf, out_sem, i - 1, 1 - slot, to_hbm=True):
                c.wait()
        for c in _tile_copies(o_hbm, obuf, out_sem, i, slot, to_hbm=True):
            c.wait()


def _rglru_layer(x, mod, norm_g, w_in, conv_w, conv_b, w_ax, b_a, b_x, lam, w_out, t_len):
    bsz, seq, d = x.shape
    assert bsz == SUBLANES and seq % t_len == 0 and t_len >= conv_w.shape[0] - 1
    n_heads, rb, _ = w_ax.shape
    full = lambda shape: pl.BlockSpec(shape, lambda i: (0,) * len(shape))
    return pl.pallas_call(
        _rglru_kernel,
        grid=(seq // t_len,),
        in_specs=[
            pl.BlockSpec(memory_space=pl.ANY),
            full((bsz, 3 * d)),
            full((1, d)),
            full((d, 2 * d)),
            full(conv_w.shape),
            full((1, d)),
            full((n_heads, rb, 2 * rb)),
            full((1, d)),
            full((1, d)),
            full((1, d)),
            full((d, d)),
        ],
        out_specs=pl.BlockSpec(memory_space=pl.ANY),
        out_shape=jax.ShapeDtypeStruct(x.shape, F32),
        scratch_shapes=[
            pltpu.VMEM((2, t_len, bsz, d), F32),
            pltpu.VMEM((2, t_len, bsz, d), F32),
            pltpu.SemaphoreType.DMA((2,)),
            pltpu.SemaphoreType.DMA((2,)),
            pltpu.VMEM((t_len, bsz, d), F32),
            pltpu.VMEM((t_len * bsz, d), BF16),
            pltpu.VMEM((conv_w.shape[0] - 1, bsz, d), F32),
            pltpu.VMEM((bsz, d), F32),
        ],
        compiler_params=pltpu.CompilerParams(
            dimension_semantics=("arbitrary",),
            vmem_limit_bytes=VMEM_LIMIT_BYTES),
        name="rglru_layer",
    )(x, mod, norm_g, w_in, conv_w, conv_b, w_ax, b_a, b_x, lam, w_out)


def _kv_kernel(x_ref, mod_ref, ng_ref, wk_ref, wvt_ref, k_ref, vt_ref, km_ref):
    d = x_ref.shape[2]
    n_heads = k_ref.shape[1]
    mod = mod_ref[0]
    h = _norm_modulate(x_ref[0], ng_ref[...], mod[:, :d], mod[:, d:]).astype(BF16)
    k = jnp.dot(h, wk_ref[...], preferred_element_type=F32)
    vt = lax.dot_general(wvt_ref[...], h, NT_DIMS, preferred_element_type=F32)
    for j in range(km_ref.shape[1]):
        km_ref[0, j] = jnp.mean(k[j * MOBA_BLOCK:(j + 1) * MOBA_BLOCK], axis=0, keepdims=True)
    for hd in range(n_heads):
        sl = slice(hd * HEAD_DIM, (hd + 1) * HEAD_DIM)
        k_ref[0, hd] = k[:, sl].astype(BF16)
        vt_ref[0, hd] = vt[sl, :].astype(BF16)


def _shared_kv(x, mod, norm_g, w_k, w_v_t, tile):
    bsz, seq, d = x.shape
    d_att = w_k.shape[1]
    n_heads = d_att // HEAD_DIM
    n_blk = seq // MOBA_BLOCK
    blk_per_tile = tile // MOBA_BLOCK
    assert tile % MOBA_BLOCK == 0 and seq % tile == 0
    return pl.pallas_call(
        _kv_kernel,
        grid=(bsz, seq // tile),
        in_specs=[
            pl.BlockSpec((1, tile, d), lambda b, i: (b, i, 0)),
            pl.BlockSpec((1, 1, 2 * d), lambda b, i: (b, 0, 0)),
            pl.BlockSpec((1, d), lambda b, i: (0, 0)),
            pl.BlockSpec((d, d_att), lambda b, i: (0, 0)),
            pl.BlockSpec((d_att, d), lambda b, i: (0, 0)),
        ],
        out_specs=[
            pl.BlockSpec((1, n_heads, tile, HEAD_DIM), lambda b, i: (b, 0, i, 0)),
            pl.BlockSpec((1, n_heads, HEAD_DIM, tile), lambda b, i: (b, 0, 0, i)),
            pl.BlockSpec((1, blk_per_tile, 1, d_att), lambda b, i: (b, i, 0, 0)),
        ],
        out_shape=[jax.ShapeDtypeStruct((bsz, n_heads, seq, HEAD_DIM), BF16),
                   jax.ShapeDtypeStruct((bsz, n_heads, HEAD_DIM, seq), BF16),
                   jax.ShapeDtypeStruct((bsz, n_blk, 1, d_att), F32)],
        compiler_params=pltpu.CompilerParams(
            dimension_semantics=("arbitrary", "arbitrary"),
            vmem_limit_bytes=VMEM_LIMIT_BYTES),
        name="shared_kv",
    )(x, mod, norm_g, w_k, w_v_t)


def _select_blocks(gate_t, n_past, sel_s):
    blk = lax.broadcasted_iota(jnp.int32, gate_t.shape, 0)
    g = jnp.where(blk < n_past, gate_t, NEG_INF)
    sel_s[...] = g
    rank = jnp.zeros(gate_t.shape, jnp.int32)
    for j in range(n_past):
        gj = jnp.broadcast_to(sel_s[j:j + 1, :], gate_t.shape)
        ahead = (gj > g) | ((gj == g) & (blk > j))
        rank = rank + ahead.astype(jnp.int32)
    sel_s[...] = jnp.where((rank < MOBA_TOPK) & (blk < n_past), 0.0, NEG_INF)


def _fold_rows(a, op):
    return op(a.reshape(a.shape[0] // SUBLANES, SUBLANES, a.shape[1]), axis=0)


def _head_scores(hd, slot, n_past, q_s, k_ref, km_ref, sel_s, sc_s, m_s):
    qf = q_s[hd]
    qh = (qf * (HEAD_DIM ** -0.5 * LOG2E)).astype(BF16)
    masked = n_past > MOBA_TOPK
    if masked:
        gate_t = lax.dot_general(km_ref[0, hd].astype(BF16), qf.astype(BF16), NT_DIMS,
                                 preferred_element_type=F32)
        _select_blocks(gate_t, n_past, sel_s)
    n_keys = (n_past + 1) * MOBA_BLOCK
    s_all = lax.dot_general(k_ref[0, hd, 0:n_keys, :], qh, NT_DIMS,
                            preferred_element_type=F32)
    m8 = None
    for j in range(n_past + 1):
        rows = slice(j * MOBA_BLOCK, (j + 1) * MOBA_BLOCK)
        s = s_all[rows]
        if j == n_past:
            key = lax.broadcasted_iota(jnp.int32, s.shape, 0)
            qry = lax.broadcasted_iota(jnp.int32, s.shape, 1)
            s = jnp.where(key <= qry, s, NEG_INF)
        elif masked:
            s = s + sel_s[j:j + 1, :]
        sc_s[slot, rows] = s
        smax = _fold_rows(s, jnp.max)
        m8 = smax if m8 is None else jnp.maximum(m8, smax)
    m_s[slot] = m8


def _head_output(hd, slot, n_past, vt_ref, sc_s, m_s, o_s):
    n_keys = (n_past + 1) * MOBA_BLOCK
    m = jnp.max(m_s[slot], axis=0, keepdims=True)
    p = jnp.exp2(sc_s[slot, 0:n_keys] - m)
    denom = jnp.sum(_fold_rows(p, jnp.sum), axis=0, keepdims=True)
    o_t = jnp.dot(vt_ref[0, hd, :, 0:n_keys], p.astype(BF16),
                  preferred_element_type=F32)
    o_s[hd] = o_t * (1.0 / denom)


def _moba_kernel(x_ref, mod_ref, ng_ref, win_ref, k_ref, vt_ref, km_ref, wout_ref, fg_ref,
                 o_ref, q_s, g_s, o_s, sel_s, sc_s, m_s, *, apply_final_norm):
    d = x_ref.shape[2]
    n_heads = k_ref.shape[1]
    n_blk = km_ref.shape[2]
    d_att = n_heads * HEAD_DIM
    qb = pl.program_id(1)
    scores = functools.partial(_head_scores, q_s=q_s, k_ref=k_ref, km_ref=km_ref, sel_s=sel_s,
                               sc_s=sc_s, m_s=m_s)
    output = functools.partial(_head_output, vt_ref=vt_ref, sc_s=sc_s, m_s=m_s, o_s=o_s)

    def block(n_past):
        x = x_ref[0]
        mod = mod_ref[0]
        shift, scale, gate = mod[:, :d], mod[:, d:2 * d], mod[:, 2 * d:]
        hb = _norm_modulate(x, ng_ref[...], shift, scale).astype(BF16)
        q = jnp.dot(hb, win_ref[:, :d_att], preferred_element_type=F32)
        for hd in range(n_heads):
            q_s[hd] = q[:, hd * HEAD_DIM:(hd + 1) * HEAD_DIM]

        scores(0, 0, n_past)
        g_s[...] = jnp.dot(hb, win_ref[:, d_att:], preferred_element_type=F32)

        def step(hd, carry):
            output(hd - 1, (hd - 1) & 1, n_past)
            scores(hd, hd & 1, n_past)
            return carry

        lax.fori_loop(1, n_heads, step, 0)
        output(n_heads - 1, (n_heads - 1) & 1, n_past)

        ys = []
        for hd in range(n_heads):
            gp = g_s[:, hd * HEAD_DIM:(hd + 1) * HEAD_DIM]
            ys.append((o_s[hd].T * (gp * _sigmoid(gp))).astype(BF16))
        early = (n_heads - 1) * HEAD_DIM // MXU_DIM * MXU_DIM
        proj = jnp.dot(jnp.concatenate(ys[:early // HEAD_DIM], axis=1), wout_ref[:early, :],
                       preferred_element_type=F32)
        proj = proj + jnp.dot(jnp.concatenate(ys[early // HEAD_DIM:], axis=1), wout_ref[early:, :],
                              preferred_element_type=F32)
        out = x + gate * proj
        if apply_final_norm:
            ms = jnp.mean(out * out, axis=-1, keepdims=True)
            out = out * lax.rsqrt(ms + EPS) * fg_ref[...]
        o_ref[0] = out

    for n_past in range(n_blk):
        pl.when(qb == n_past)(functools.partial(block, n_past))


def _moba_layer(x, mod, norm_g, w_in, k, v_t, k_mean, w_out, final_g, apply_final_norm):
    bsz, seq, d = x.shape
    n_heads = k.shape[1]
    d_att = n_heads * HEAD_DIM
    n_blk = seq // MOBA_BLOCK
    full = lambda shape: pl.BlockSpec(shape, lambda b, i: (0,) * len(shape))
    per_batch = lambda shape: pl.BlockSpec(shape, lambda b, i: (b,) + (0,) * (len(shape) - 1))
    return pl.pallas_call(
        functools.partial(_moba_kernel, apply_final_norm=apply_final_norm),
        grid=(bsz, n_blk),
        in_specs=[
            pl.BlockSpec((1, MOBA_BLOCK, d), lambda b, i: (b, i, 0)),
            per_batch((1, 1, 3 * d)),
            full((1, d)),
            full((d, 2 * d_att)),
            per_batch((1, n_heads, seq, HEAD_DIM)),
            per_batch((1, n_heads, HEAD_DIM, seq)),
            per_batch((1, n_heads, n_blk, HEAD_DIM)),
            full((d_att, d)),
            full((1, d)),
        ],
        out_specs=pl.BlockSpec((1, MOBA_BLOCK, d), lambda b, i: (b, i, 0)),
        out_shape=jax.ShapeDtypeStruct(x.shape, F32),
        scratch_shapes=[
            pltpu.VMEM((n_heads, MOBA_BLOCK, HEAD_DIM), F32),
            pltpu.VMEM((MOBA_BLOCK, d_att), F32),
            pltpu.VMEM((n_heads, HEAD_DIM, MOBA_BLOCK), F32),
            pltpu.VMEM((n_blk, MOBA_BLOCK), F32),
            pltpu.VMEM((2, seq, MOBA_BLOCK), F32),
            pltpu.VMEM((2, SUBLANES, MOBA_BLOCK), F32),
        ],
        compiler_params=pltpu.CompilerParams(
            dimension_semantics=("arbitrary", "arbitrary"),
            vmem_limit_bytes=VMEM_LIMIT_BYTES),
        name="moba_layer",
    )(x, mod, norm_g, w_in, k, v_t, k_mean, w_out, final_g)


def kernel(x, c, mod_w, mod_b, norm_g, rg_w_in, rg_conv_w, rg_conv_b, rg_w_a, rg_b_a, rg_w_x,
           rg_b_x, rg_lambda, rg_w_out, kv_norm_g, kv_mod_w, kv_mod_b, w_kv, att_w_in,
           att_w_out, final_norm_g):
    bsz, seq, d = x.shape
    depth = mod_w.shape[0]
    n_a = rg_w_in.shape[0]
    n_b = att_w_in.shape[0]
    assert depth == n_a + n_b and seq % MOBA_BLOCK == 0
    d_att = w_kv.shape[1] // 2
    n_heads = d_att // HEAD_DIM
    n_blk = seq // MOBA_BLOCK

    mod = _modulation(c, mod_w, mod_b, tn=768)
    kv_mod = _modulation(c, kv_mod_w[None], kv_mod_b[None], tn=1024)

    row = lambda p: p.reshape(1, -1)
    for i in range(n_a):
        w_ax = jnp.concatenate([rg_w_a[i], rg_w_x[i]], axis=-1).astype(BF16)
        x = _rglru_layer(
            x, mod[i], row(norm_g[i]), rg_w_in[i].astype(BF16), rg_conv_w[i],
            row(rg_conv_b[i]), w_ax, row(rg_b_a[i]), row(rg_b_x[i]), row(rg_lambda[i]),
            rg_w_out[i].astype(BF16), t_len=64)

    k, v_t, k_mean = _shared_kv(x, kv_mod[0][:, None, :], row(kv_norm_g),
                                w_kv[:, :d_att].astype(BF16), w_kv[:, d_att:].T.astype(BF16),
                                tile=2 * MOBA_BLOCK)
    k_mean = k_mean.reshape(bsz, n_blk, n_heads, HEAD_DIM).transpose(0, 2, 1, 3)

    for i in range(n_b):
        layer = n_a + i
        x = _moba_layer(
            x, mod[layer][:, None, :], row(norm_g[layer]), att_w_in[i].astype(BF16), k, v_t,
            k_mean, att_w_out[i].astype(BF16), row(final_norm_g),
            apply_final_norm=(i == n_b - 1))
    return x
```

```python
import functools

import jax
import jax.numpy as jnp
from jax import lax
from jax.experimental import pallas as pl
from jax.experimental.pallas import tpu as pltpu

EPS = 1e-6
RG_C = 8.0
HEAD_DIM = 128
MOBA_BLOCK = 256
MOBA_TOPK = 3
NEG_INF = -1e30
LOG2E = 1.4426950408889634
SUBLANES = 8
MXU_DIM = 256
VMEM_LIMIT_BYTES = 56 * 1024 * 1024

F32 = jnp.float32
BF16 = jnp.bfloat16
NT_DIMS = (((1,), (1,)), ((), ()))


def _sigmoid(z):
    return 1.0 / (1.0 + jnp.exp2(z * (-LOG2E)))


def _sqrt_nonneg(v):
    return jnp.where(v > 0.0, v * lax.rsqrt(v), 0.0)


def _norm_modulate(x, norm_g, shift, scale):
    ms = jnp.mean(x * x, axis=-1, keepdims=True)
    return x * lax.rsqrt(ms + EPS) * (norm_g * (1.0 + scale)) + shift


def _mod_kernel(c_ref, w_ref, b_ref, o_ref):
    c = c_ref[...]
    cs = (c * _sigmoid(c)).astype(BF16)
    w = w_ref[0].astype(BF16)
    o_ref[0] = jnp.dot(cs, w, preferred_element_type=F32) + b_ref[0]


def _modulation(c, w, b, tn):
    n_layers, d, n = w.shape
    bsz = c.shape[0]
    return pl.pallas_call(
        _mod_kernel,
        grid=(n_layers, n // tn),
        in_specs=[
            pl.BlockSpec((bsz, d), lambda l, j: (0, 0)),
            pl.BlockSpec((1, d, tn), lambda l, j: (l, 0, j)),
            pl.BlockSpec((1, 1, tn), lambda l, j: (l, 0, j)),
        ],
        out_specs=pl.BlockSpec((1, bsz, tn), lambda l, j: (l, 0, j)),
        out_shape=jax.ShapeDtypeStruct((n_layers, bsz, n), F32),
        compiler_params=pltpu.CompilerParams(
            dimension_semantics=("arbitrary", "arbitrary"),
            vmem_limit_bytes=VMEM_LIMIT_BYTES),
        name="adaln_mod",
    )(c, w, b.reshape(n_layers, 1, n))


def _tile_copies(hbm_ref, buf, sem, tile, slot, to_hbm):
    t_len, bsz, _ = buf.shape[1:]
    copies = []
    for b in range(bsz):
        hbm = hbm_ref.at[b, pl.ds(tile * t_len, t_len), :]
        vmem = buf.at[slot, :, b, :]
        src, dst = (vmem, hbm) if to_hbm else (hbm, vmem)
        copies.append(pltpu.make_async_copy(src, dst, sem.at[slot]))
    return copies


def _rglru_kernel(x_hbm, mod_ref, ng_ref, win_ref, cw_ref, cb_ref, wax_ref, ba_ref,
                  bx_ref, lam_ref, wout_ref, o_hbm,
                  xbuf, obuf, in_sem, out_sem, hs_s, y_s, utail, hstate):
    _, t_len, bsz, d = xbuf.shape
    rows = t_len * bsz
    n_heads, rb, _ = wax_ref.shape
    conv_w = cw_ref.shape[0]
    i = pl.program_id(0)
    n_tiles = x_hbm.shape[1] // t_len
    slot = i & 1

    @pl.when(i == 0)
    def _():
        for c in _tile_copies(x_hbm, xbuf, in_sem, 0, 0, to_hbm=False):
            c.start()
        utail[...] = jnp.zeros_like(utail)
        hstate[...] = jnp.zeros_like(hstate)

    @pl.when(i + 1 < n_tiles)
    def _():
        for c in _tile_copies(x_hbm, xbuf, in_sem, i + 1, 1 - slot, to_hbm=False):
            c.start()

    for c in _tile_copies(x_hbm, xbuf, in_sem, i, slot, to_hbm=False):
        c.wait()

    x3 = xbuf[slot]
    mod = mod_ref[...]
    shift, scale, gate = mod[:, :d], mod[:, d:2 * d], mod[:, 2 * d:]
    ms = jnp.mean(x3 * x3, axis=-1, keepdims=True)
    h3 = x3 * lax.rsqrt(ms + EPS) * (ng_ref[...] * (1.0 + scale)) + shift
    hb = h3.reshape(rows, d).astype(BF16)
    u3 = jnp.dot(hb, win_ref[:, :d], preferred_element_type=F32).reshape(t_len, bsz, d)

    upad = jnp.concatenate([utail[...], u3], axis=0)
    utail[...] = u3[t_len - (conv_w - 1):]
    uc3 = cb_ref[...] + cw_ref[conv_w - 1:conv_w, :] * u3
    for k in range(conv_w - 1):
        uc3 = uc3 + cw_ref[k:k + 1, :] * upad[k:k + t_len]
    uc = uc3.reshape(rows, d)

    lam = lam_ref[...]
    softplus_neg_lam = jnp.maximum(-lam, 0.0) + jnp.log1p(jnp.exp(-jnp.abs(lam)))
    log2_a_per_r = (-RG_C * LOG2E) * softplus_neg_lam
    for hh in range(n_heads):
        sl = slice(hh * rb, (hh + 1) * rb)
        uch = uc[:, sl]
        z = jnp.dot(uch.astype(BF16), wax_ref[hh], preferred_element_type=F32)
        r = _sigmoid(z[:, :rb] + ba_ref[:, sl])
        gi = _sigmoid(z[:, rb:] + bx_ref[:, sl])
        a = jnp.exp2(r * log2_a_per_r[:, sl])
        b_in = _sqrt_nonneg(1.0 - a * a) * (gi * uch)
        a3 = a.reshape(t_len, bsz, rb)
        b3 = b_in.reshape(t_len, bsz, rb)
        h_run = hstate[:, sl]
        for t in range(t_len):
            h_run = a3[t] * h_run + b3[t]
            hs_s[t, :, sl] = h_run
        hstate[:, sl] = h_run
        gpath = jnp.dot(hb, win_ref[:, d + hh * rb:d + (hh + 1) * rb], preferred_element_type=F32)
        y = hs_s[:, :, sl].reshape(rows, rb) * (gpath * _sigmoid(gpath))
        y_s[:, sl] = y.astype(BF16)

    proj = jnp.dot(y_s[...], wout_ref[...], preferred_element_type=F32)
    out3 = x3 + gate * proj.reshape(t_len, bsz, d)

    @pl.when(i >= 2)
    def _():
        for c in _tile_copies(o_hbm, obuf, out_sem, i - 2, slot, to_hbm=True):
            c.wait()

    obuf[slot] = out3
    for c in _tile_copies(o_hbm, obuf, out_sem, i, slot, to_hbm=True):
        c.start()

    @pl.when(i == n_tiles - 1)
    def _():
        if n_tiles >= 2:
            for c in _tile_copies(o_hbm, obuf, out_sem, i - 1, 1 - slot, to_hbm=True):
                c.wait()
        for c in _tile_copies(o_hbm, obuf, out_sem, i, slot, to_hbm=True):
            c.wait()


def _rglru_layer(x, mod, norm_g, w_in, conv_w, conv_b, w_ax, b_a, b_x, lam, w_out, t_len):
    bsz, seq, d = x.shape
    assert bsz == SUBLANES and seq % t_len == 0 and t_len >= conv_w.shape[0] - 1
    n_heads, rb, _ = w_ax.shape
    full = lambda shape: pl.BlockSpec(shape, lambda i: (0,) * len(shape))
    return pl.pallas_call(
        _rglru_kernel,
        grid=(seq // t_len,),
        in_specs=[
            pl.BlockSpec(memory_space=pl.ANY),
            full((bsz, 3 * d)),
            full((1, d)),
            full((d, 2 * d)),
            full(conv_w.shape),
            full((1, d)),
            full((n_heads, rb, 2 * rb)),
            full((1, d)),
            full((1, d)),
            full((1, d)),
            full((d, d)),
        ],
        out_specs=pl.BlockSpec(memory_space=pl.ANY),
        out_shape=jax.ShapeDtypeStruct(x.shape, F32),
        scratch_shapes=[
            pltpu.VMEM((2, t_len, bsz, d), F32),
            pltpu.VMEM((2, t_len, bsz, d), F32),
            pltpu.SemaphoreType.DMA((2,)),
            pltpu.SemaphoreType.DMA((2,)),
            pltpu.VMEM((t_len, bsz, d), F32),
            pltpu.VMEM((t_len * bsz, d), BF16),
            pltpu.VMEM((conv_w.shape[0] - 1, bsz, d), F32),
            pltpu.VMEM((bsz, d), F32),
        ],
        compiler_params=pltpu.CompilerParams(
            dimension_semantics=("arbitrary",),
            vmem_limit_bytes=VMEM_LIMIT_BYTES),
        name="rglru_layer",
    )(x, mod, norm_g, w_in, conv_w, conv_b, w_ax, b_a, b_x, lam, w_out)


def _kv_kernel(x_ref, mod_ref, ng_ref, wk_ref, wvt_ref, k_ref, vt_ref, km_ref):
    d = x_ref.shape[2]
    n_heads = k_ref.shape[1]
    mod = mod_ref[0]
    h = _norm_modulate(x_ref[0], ng_ref[...], mod[:, :d], mod[:, d:]).astype(BF16)
    k = jnp.dot(h, wk_ref[...], preferred_element_type=F32)
    vt = lax.dot_general(wvt_ref[...], h, NT_DIMS, preferred_element_type=F32)
    for j in range(km_ref.shape[1]):
        km_ref[0, j] = jnp.mean(k[j * MOBA_BLOCK:(j + 1) * MOBA_BLOCK], axis=0, keepdims=True)
    for hd in range(n_heads):
        sl = slice(hd * HEAD_DIM, (hd + 1) * HEAD_DIM)
        k_ref[0, hd] = k[:, sl].astype(BF16)
        vt_ref[0, hd] = vt[sl, :].astype(BF16)


def _shared_kv(x, mod, norm_g, w_k, w_v_t, tile):
    bsz, seq, d = x.shape
    d_att = w_k.shape[1]
    n_heads = d_att // HEAD_DIM
    n_blk = seq // MOBA_BLOCK
    blk_per_tile = tile // MOBA_BLOCK
    assert tile % MOBA_BLOCK == 0 and seq % tile == 0
    return pl.pallas_call(
        _kv_kernel,
        grid=(bsz, seq // tile),
        in_specs=[
            pl.BlockSpec((1, tile, d), lambda b, i: (b, i, 0)),
            pl.BlockSpec((1, 1, 2 * d), lambda b, i: (b, 0, 0)),
            pl.BlockSpec((1, d), lambda b, i: (0, 0)),
            pl.BlockSpec((d, d_att), lambda b, i: (0, 0)),
            pl.BlockSpec((d_att, d), lambda b, i: (0, 0)),
        ],
        out_specs=[
            pl.BlockSpec((1, n_heads, tile, HEAD_DIM), lambda b, i: (b, 0, i, 0)),
            pl.BlockSpec((1, n_heads, HEAD_DIM, tile), lambda b, i: (b, 0, 0, i)),
            pl.BlockSpec((1, blk_per_tile, 1, d_att), lambda b, i: (b, i, 0, 0)),
        ],
        out_shape=[jax.ShapeDtypeStruct((bsz, n_heads, seq, HEAD_DIM), BF16),
                   jax.ShapeDtypeStruct((bsz, n_heads, HEAD_DIM, seq), BF16),
                   jax.ShapeDtypeStruct((bsz, n_blk, 1, d_att), F32)],
        compiler_params=pltpu.CompilerParams(
            dimension_semantics=("arbitrary", "arbitrary"),
            vmem_limit_bytes=VMEM_LIMIT_BYTES),
        name="shared_kv",
    )(x, mod, norm_g, w_k, w_v_t)


def _select_blocks(gate_t, n_past, sel_s):
    blk = lax.broadcasted_iota(jnp.int32, gate_t.shape, 0)
    g = jnp.where(blk < n_past, gate_t, NEG_INF)
    sel_s[...] = g
    rank = jnp.zeros(gate_t.shape, jnp.int32)
    for j in range(n_past):
        gj = jnp.broadcast_to(sel_s[j:j + 1, :], gate_t.shape)
        ahead = (gj > g) | ((gj == g) & (blk > j))
        rank = rank + ahead.astype(jnp.int32)
    sel_s[...] = jnp.where((rank < MOBA_TOPK) & (blk < n_past), 0.0, NEG_INF)


def _fold_rows(a, op):
    return op(a.reshape(a.shape[0] // SUBLANES, SUBLANES, a.shape[1]), axis=0)


MAX_HEADS_PER_STAGE = 8


def _heads_per_stage(n_past, slot_rows):
    per_stage = 1
    while (per_stage < MAX_HEADS_PER_STAGE
           and 2 * per_stage * (n_past + 1) * MOBA_BLOCK <= slot_rows):
        per_stage *= 2
    return per_stage


def _head_scores(hd, slot, lane, n_past, q_s, k_ref, km_ref, sel_s, sc_s, m_s):
    qf = q_s[hd]
    qh = (qf * (HEAD_DIM ** -0.5 * LOG2E)).astype(BF16)
    masked = n_past > MOBA_TOPK
    if masked:
        gate_t = lax.dot_general(km_ref[0, hd].astype(BF16), qf.astype(BF16), NT_DIMS,
                                 preferred_element_type=F32)
        sel_s = sel_s.at[lane]
        _select_blocks(gate_t, n_past, sel_s)
    n_keys = (n_past + 1) * MOBA_BLOCK
    s_all = lax.dot_general(k_ref[0, hd, 0:n_keys, :], qh, NT_DIMS,
                            preferred_element_type=F32)
    m8 = None
    for j in range(n_past + 1):
        rows = slice(j * MOBA_BLOCK, (j + 1) * MOBA_BLOCK)
        s = s_all[rows]
        if j == n_past:
            key = lax.broadcasted_iota(jnp.int32, s.shape, 0)
            qry = lax.broadcasted_iota(jnp.int32, s.shape, 1)
            s = jnp.where(key <= qry, s, NEG_INF)
        elif masked:
            s = s + sel_s[j:j + 1, :]
        sc_s[slot, lane * n_keys + j * MOBA_BLOCK:lane * n_keys + (j + 1) * MOBA_BLOCK] = s
        smax = _fold_rows(s, jnp.max)
        m8 = smax if m8 is None else jnp.maximum(m8, smax)
    m_s[slot, lane] = m8


def _head_output(hd, slot, lane, n_past, vt_ref, sc_s, m_s, o_s):
    n_keys = (n_past + 1) * MOBA_BLOCK
    m = jnp.max(m_s[slot, lane], axis=0, keepdims=True)
    p = jnp.exp2(sc_s[slot, lane * n_keys:(lane + 1) * n_keys] - m)
    denom = jnp.sum(_fold_rows(p, jnp.sum), axis=0, keepdims=True)
    o_t = jnp.dot(vt_ref[0, hd, :, 0:n_keys], p.astype(BF16),
                  preferred_element_type=F32)
    o_s[hd] = o_t * (1.0 / denom)


def _moba_kernel(x_ref, mod_ref, ng_ref, win_ref, k_ref, vt_ref, km_ref, wout_ref, fg_ref,
                 o_ref, q_s, g_s, o_s, sel_s, sc_s, m_s, *, apply_final_norm):
    d = x_ref.shape[2]
    n_heads = k_ref.shape[1]
    n_blk = km_ref.shape[2]
    d_att = n_heads * HEAD_DIM
    qb = pl.program_id(1)
    scores = functools.partial(_head_scores, q_s=q_s, k_ref=k_ref, km_ref=km_ref, sel_s=sel_s,
                               sc_s=sc_s, m_s=m_s)
    output = functools.partial(_head_output, vt_ref=vt_ref, sc_s=sc_s, m_s=m_s, o_s=o_s)

    def block(n_past):
        x = x_ref[0]
        mod = mod_ref[0]
        shift, scale, gate = mod[:, :d], mod[:, d:2 * d], mod[:, 2 * d:]
        hb = _norm_modulate(x, ng_ref[...], shift, scale).astype(BF16)
        q = jnp.dot(hb, win_ref[:, :d_att], preferred_element_type=F32)
        for hd in range(n_heads):
            q_s[hd] = q[:, hd * HEAD_DIM:(hd + 1) * HEAD_DIM]

        per_stage = _heads_per_stage(n_past, sc_s.shape[1])
        n_groups = n_heads // per_stage

        def stage(fn, group):
            for lane in range(per_stage):
                fn(group * per_stage + lane, group & 1, lane, n_past)

        stage(scores, 0)
        g_s[...] = jnp.dot(hb, win_ref[:, d_att:], preferred_element_type=F32)

        def step(group, carry):
            stage(output, group - 1)
            stage(scores, group)
            return carry

        lax.fori_loop(1, n_groups, step, 0)
        stage(output, n_groups - 1)

        ys = []
        for hd in range(n_heads):
            gp = g_s[:, hd * HEAD_DIM:(hd + 1) * HEAD_DIM]
            ys.append((o_s[hd].T * (gp * _sigmoid(gp))).astype(BF16))
        early = (n_heads - 1) * HEAD_DIM // MXU_DIM * MXU_DIM
        proj = jnp.dot(jnp.concatenate(ys[:early // HEAD_DIM], axis=1), wout_ref[:early, :],
                       preferred_element_type=F32)
        proj = proj + jnp.dot(jnp.concatenate(ys[early // HEAD_DIM:], axis=1), wout_ref[early:, :],
                              preferred_element_type=F32)
        out = x + gate * proj
        if apply_final_norm:
            ms = jnp.mean(out * out, axis=-1, keepdims=True)
            out = out * lax.rsqrt(ms + EPS) * fg_ref[...]
        o_ref[0] = out

    for n_past in range(n_blk):
        pl.when(qb == n_past)(functools.partial(block, n_past))


def _moba_layer(x, mod, norm_g, w_in, k, v_t, k_mean, w_out, final_g, apply_final_norm):
    bsz, seq, d = x.shape
    n_heads = k.shape[1]
    d_att = n_heads * HEAD_DIM
    n_blk = seq // MOBA_BLOCK
    full = lambda shape: pl.BlockSpec(shape, lambda b, i: (0,) * len(shape))
    per_batch = lambda shape: pl.BlockSpec(shape, lambda b, i: (b,) + (0,) * (len(shape) - 1))
    return pl.pallas_call(
        functools.partial(_moba_kernel, apply_final_norm=apply_final_norm),
        grid=(bsz, n_blk),
        in_specs=[
            pl.BlockSpec((1, MOBA_BLOCK, d), lambda b, i: (b, i, 0)),
            per_batch((1, 1, 3 * d)),
            full((1, d)),
            full((d, 2 * d_att)),
            per_batch((1, n_heads, seq, HEAD_DIM)),
            per_batch((1, n_heads, HEAD_DIM, seq)),
            per_batch((1, n_heads, n_blk, HEAD_DIM)),
            full((d_att, d)),
            full((1, d)),
        ],
        out_specs=pl.BlockSpec((1, MOBA_BLOCK, d), lambda b, i: (b, i, 0)),
        out_shape=jax.ShapeDtypeStruct(x.shape, F32),
        scratch_shapes=[
            pltpu.VMEM((n_heads, MOBA_BLOCK, HEAD_DIM), F32),
            pltpu.VMEM((MOBA_BLOCK, d_att), F32),
            pltpu.VMEM((n_heads, HEAD_DIM, MOBA_BLOCK), F32),
            pltpu.VMEM((MAX_HEADS_PER_STAGE, n_blk, MOBA_BLOCK), F32),
            pltpu.VMEM((2, 2 * seq, MOBA_BLOCK), F32),
            pltpu.VMEM((2, MAX_HEADS_PER_STAGE, SUBLANES, MOBA_BLOCK), F32),
        ],
        compiler_params=pltpu.CompilerParams(
            dimension_semantics=("arbitrary", "arbitrary"),
            vmem_limit_bytes=VMEM_LIMIT_BYTES),
        name="moba_layer",
    )(x, mod, norm_g, w_in, k, v_t, k_mean, w_out, final_g)


def kernel(x, c, mod_w, mod_b, norm_g, rg_w_in, rg_conv_w, rg_conv_b, rg_w_a, rg_b_a, rg_w_x,
           rg_b_x, rg_lambda, rg_w_out, kv_norm_g, kv_mod_w, kv_mod_b, w_kv, att_w_in,
           att_w_out, final_norm_g):
    bsz, seq, d = x.shape
    depth = mod_w.shape[0]
    n_a = rg_w_in.shape[0]
    n_b = att_w_in.shape[0]
    assert depth == n_a + n_b and seq % MOBA_BLOCK == 0
    d_att = w_kv.shape[1] // 2
    n_heads = d_att // HEAD_DIM
    n_blk = seq // MOBA_BLOCK

    mod = _modulation(c, mod_w, mod_b, tn=768)
    kv_mod = _modulation(c, kv_mod_w[None], kv_mod_b[None], tn=1024)

    row = lambda p: p.reshape(1, -1)
    for i in range(n_a):
        w_ax = jnp.concatenate([rg_w_a[i], rg_w_x[i]], axis=-1).astype(BF16)
        x = _rglru_layer(
            x, mod[i], row(norm_g[i]), rg_w_in[i].astype(BF16), rg_conv_w[i],
            row(rg_conv_b[i]), w_ax, row(rg_b_a[i]), row(rg_b_x[i]), row(rg_lambda[i]),
            rg_w_out[i].astype(BF16), t_len=128)

    k, v_t, k_mean = _shared_kv(x, kv_mod[0][:, None, :], row(kv_norm_g),
                                w_kv[:, :d_att].astype(BF16), w_kv[:, d_att:].T.astype(BF16),
                                tile=2 * MOBA_BLOCK)
    k_mean = k_mean.reshape(bsz, n_blk, n_heads, HEAD_DIM).transpose(0, 2, 1, 3)

    for i in range(n_b):
        layer = n_a + i
        x = _moba_layer(
            x, mod[layer][:, None, :], row(norm_g[layer]), att_w_in[i].astype(BF16), k, v_t,
            k_mean, att_w_out[i].astype(BF16), row(final_norm_g),
            apply_final_norm=(i == n_b - 1))
    return x
```

```python
import functools

import jax
import jax.numpy as jnp
from jax import lax
from jax.experimental import pallas as pl
from jax.experimental.pallas import tpu as pltpu

EPS = 1e-6
RG_C = 8.0
HEAD_DIM = 128
MOBA_BLOCK = 256
MOBA_TOPK = 3
NEG_INF = -1e30
LOG2E = 1.4426950408889634
SUBLANES = 8
MXU_DIM = 256
VMEM_LIMIT_BYTES = 56 * 1024 * 1024

F32 = jnp.float32
BF16 = jnp.bfloat16
NT_DIMS = (((1,), (1,)), ((), ()))


def _sigmoid(z):
    return 1.0 / (1.0 + jnp.exp2(z * (-LOG2E)))


def _sqrt_nonneg(v):
    return jnp.where(v > 0.0, v * lax.rsqrt(v), 0.0)


def _norm_modulate(x, norm_g, shift, scale):
    ms = jnp.mean(x * x, axis=-1, keepdims=True)
    return x * lax.rsqrt(ms + EPS) * (norm_g * (1.0 + scale)) + shift


def _mod_kernel(c_ref, w_ref, b_ref, o_ref):
    c = c_ref[...]
    cs = (c * _sigmoid(c)).astype(BF16)
    w = w_ref[0].astype(BF16)
    o_ref[0] = jnp.dot(cs, w, preferred_element_type=F32) + b_ref[0]


def _modulation(c, w, b, tn):
    n_layers, d, n = w.shape
    bsz = c.shape[0]
    return pl.pallas_call(
        _mod_kernel,
        grid=(n_layers, n // tn),
        in_specs=[
            pl.BlockSpec((bsz, d), lambda l, j: (0, 0)),
            pl.BlockSpec((1, d, tn), lambda l, j: (l, 0, j)),
            pl.BlockSpec((1, 1, tn), lambda l, j: (l, 0, j)),
        ],
        out_specs=pl.BlockSpec((1, bsz, tn), lambda l, j: (l, 0, j)),
        out_shape=jax.ShapeDtypeStruct((n_layers, bsz, n), F32),
        compiler_params=pltpu.CompilerParams(
            dimension_semantics=("arbitrary", "arbitrary"),
            vmem_limit_bytes=VMEM_LIMIT_BYTES),
        name="adaln_mod",
    )(c, w, b.reshape(n_layers, 1, n))


def _tile_copies(hbm_ref, buf, sem, tile, slot, to_hbm):
    t_len, bsz, _ = buf.shape[1:]
    copies = []
    for b in range(bsz):
        hbm = hbm_ref.at[b, pl.ds(tile * t_len, t_len), :]
        vmem = buf.at[slot, :, b, :]
        src, dst = (vmem, hbm) if to_hbm else (hbm, vmem)
        copies.append(pltpu.make_async_copy(src, dst, sem.at[slot]))
    return copies


def _rglru_kernel(x_hbm, mod_ref, ng_ref, win_ref, cw_ref, cb_ref, wax_ref, ba_ref,
                  bx_ref, lam_ref, wout_ref, o_hbm,
                  xbuf, obuf, in_sem, out_sem, hs_s, y_s, utail, hstate):
    _, t_len, bsz, d = xbuf.shape
    rows = t_len * bsz
    n_heads, rb, _ = wax_ref.shape
    conv_w = cw_ref.shape[0]
    i = pl.program_id(0)
    n_tiles = x_hbm.shape[1] // t_len
    slot = i & 1

    @pl.when(i == 0)
    def _():
        for c in _tile_copies(x_hbm, xbuf, in_sem, 0, 0, to_hbm=False):
            c.start()
        utail[...] = jnp.zeros_like(utail)
        hstate[...] = jnp.zeros_like(hstate)

    @pl.when(i + 1 < n_tiles)
    def _():
        for c in _tile_copies(x_hbm, xbuf, in_sem, i + 1, 1 - slot, to_hbm=False):
            c.start()

    for c in _tile_copies(x_hbm, xbuf, in_sem, i, slot, to_hbm=False):
        c.wait()

    x3 = xbuf[slot]
    mod = mod_ref[...]
    shift, scale, gate = mod[:, :d], mod[:, d:2 * d], mod[:, 2 * d:]
    ms = jnp.mean(x3 * x3, axis=-1, keepdims=True)
    h3 = x3 * lax.rsqrt(ms + EPS) * (ng_ref[...] * (1.0 + scale)) + shift
    hb = h3.reshape(rows, d).astype(BF16)
    u3 = jnp.dot(hb, win_ref[:, :d], preferred_element_type=F32).reshape(t_len, bsz, d)

    upad = jnp.concatenate([utail[...], u3], axis=0)
    utail[...] = u3[t_len - (conv_w - 1):]
    uc3 = cb_ref[...] + cw_ref[conv_w - 1:conv_w, :] * u3
    for k in range(conv_w - 1):
        uc3 = uc3 + cw_ref[k:k + 1, :] * upad[k:k + t_len]
    uc = uc3.reshape(rows, d)

    lam = lam_ref[...]
    softplus_neg_lam = jnp.maximum(-lam, 0.0) + jnp.log1p(jnp.exp(-jnp.abs(lam)))
    log2_a_per_r = (-RG_C * LOG2E) * softplus_neg_lam
    for hh in range(n_heads):
        sl = slice(hh * rb, (hh + 1) * rb)
        uch = uc[:, sl]
        z = jnp.dot(uch.astype(BF16), wax_ref[hh], preferred_element_type=F32)
        r = _sigmoid(z[:, :rb] + ba_ref[:, sl])
        gi = _sigmoid(z[:, rb:] + bx_ref[:, sl])
        a = jnp.exp2(r * log2_a_per_r[:, sl])
        b_in = _sqrt_nonneg(1.0 - a * a) * (gi * uch)
        a3 = a.reshape(t_len, bsz, rb)
        b3 = b_in.reshape(t_len, bsz, rb)
        h_run = hstate[:, sl]
        for t in range(t_len):
            h_run = a3[t] * h_run + b3[t]
            hs_s[t, :, sl] = h_run
        hstate[:, sl] = h_run
        gpath = jnp.dot(hb, win_ref[:, d + hh * rb:d + (hh + 1) * rb], preferred_element_type=F32)
        y = hs_s[:, :, sl].reshape(rows, rb) * (gpath * _sigmoid(gpath))
        y_s[:, sl] = y.astype(BF16)

    proj = jnp.dot(y_s[...], wout_ref[...], preferred_element_type=F32)
    out3 = x3 + gate * proj.reshape(t_len, bsz, d)

    @pl.when(i >= 2)
    def _():
        for c in _tile_copies(o_hbm, obuf, out_sem, i - 2, slot, to_hbm=True):
            c.wait()

    obuf[slot] = out3
    for c in _tile_copies(o_hbm, obuf, out_sem, i, slot, to_hbm=True):
        c.start()

    @pl.when(i == n_tiles - 1)
    def _():
        if n_tiles >= 2:
            for c in _tile_copies(o_hbm, obuf, out_sem, i - 1, 1 - slot, to_hbm=True):
                c.wait()
        for c in _tile_copies(o_hbm, obuf, out_sem, i, slot, to_hbm=True):
            c.wait()


def _rglru_layer(x, mod, norm_g, w_in, conv_w, conv_b, w_ax, b_a, b_x, lam, w_out, t_len):
    bsz, seq, d = x.shape
    assert bsz == SUBLANES and seq % t_len == 0 and t_len >= conv_w.shape[0] - 1
    n_heads, rb, _ = w_ax.shape
    full = lambda shape: pl.BlockSpec(shape, lambda i: (0,) * len(shape))
    return pl.pallas_call(
        _rglru_kernel,
        grid=(seq // t_len,),
        in_specs=[
            pl.BlockSpec(memory_space=pl.ANY),
            full((bsz, 3 * d)),
            full((1, d)),
            full((d, 2 * d)),
            full(conv_w.shape),
            full((1, d)),
            full((n_heads, rb, 2 * rb)),
            full((1, d)),
            full((1, d)),
            full((1, d)),
            full((d, d)),
        ],
        out_specs=pl.BlockSpec(memory_space=pl.ANY),
        out_shape=jax.ShapeDtypeStruct(x.shape, F32),
        scratch_shapes=[
            pltpu.VMEM((2, t_len, bsz, d), F32),
            pltpu.VMEM((2, t_len, bsz, d), F32),
            pltpu.SemaphoreType.DMA((2,)),
            pltpu.SemaphoreType.DMA((2,)),
            pltpu.VMEM((t_len, bsz, d), F32),
            pltpu.VMEM((t_len * bsz, d), BF16),
            pltpu.VMEM((conv_w.shape[0] - 1, bsz, d), F32),
            pltpu.VMEM((bsz, d), F32),
        ],
        compiler_params=pltpu.CompilerParams(
            dimension_semantics=("arbitrary",),
            vmem_limit_bytes=VMEM_LIMIT_BYTES),
        name="rglru_layer",
    )(x, mod, norm_g, w_in, conv_w, conv_b, w_ax, b_a, b_x, lam, w_out)


def _kv_kernel(x_ref, mod_ref, ng_ref, wk_ref, wvt_ref, k_ref, vt_ref, km_ref):
    d = x_ref.shape[2]
    n_heads = k_ref.shape[1]
    mod = mod_ref[0]
    h = _norm_modulate(x_ref[0], ng_ref[...], mod[:, :d], mod[:, d:]).astype(BF16)
    k = jnp.dot(h, wk_ref[...], preferred_element_type=F32)
    vt = lax.dot_general(wvt_ref[...], h, NT_DIMS, preferred_element_type=F32)
    for j in range(km_ref.shape[1]):
        km_ref[0, j] = jnp.mean(k[j * MOBA_BLOCK:(j + 1) * MOBA_BLOCK], axis=0, keepdims=True)
    for hd in range(n_heads):
        sl = slice(hd * HEAD_DIM, (hd + 1) * HEAD_DIM)
        k_ref[0, hd] = k[:, sl].astype(BF16)
        vt_ref[0, hd] = vt[sl, :].astype(BF16)


def _shared_kv(x, mod, norm_g, w_k, w_v_t, tile):
    bsz, seq, d = x.shape
    d_att = w_k.shape[1]
    n_heads = d_att // HEAD_DIM
    n_blk = seq // MOBA_BLOCK
    blk_per_tile = tile // MOBA_BLOCK
    assert tile % MOBA_BLOCK == 0 and seq % tile == 0
    return pl.pallas_call(
        _kv_kernel,
        grid=(bsz, seq // tile),
        in_specs=[
            pl.BlockSpec((1, tile, d), lambda b, i: (b, i, 0)),
            pl.BlockSpec((1, 1, 2 * d), lambda b, i: (b, 0, 0)),
            pl.BlockSpec((1, d), lambda b, i: (0, 0)),
            pl.BlockSpec((d, d_att), lambda b, i: (0, 0)),
            pl.BlockSpec((d_att, d), lambda b, i: (0, 0)),
        ],
        out_specs=[
            pl.BlockSpec((1, n_heads, tile, HEAD_DIM), lambda b, i: (b, 0, i, 0)),
            pl.BlockSpec((1, n_heads, HEAD_DIM, tile), lambda b, i: (b, 0, 0, i)),
            pl.BlockSpec((1, blk_per_tile, 1, d_att), lambda b, i: (b, i, 0, 0)),
        ],
        out_shape=[jax.ShapeDtypeStruct((bsz, n_heads, seq, HEAD_DIM), BF16),
                   jax.ShapeDtypeStruct((bsz, n_heads, HEAD_DIM, seq), BF16),
                   jax.ShapeDtypeStruct((bsz, n_blk, 1, d_att), F32)],
        compiler_params=pltpu.CompilerParams(
            dimension_semantics=("arbitrary", "arbitrary"),
            vmem_limit_bytes=VMEM_LIMIT_BYTES),
        name="shared_kv",
    )(x, mod, norm_g, w_k, w_v_t)


def _select_blocks(gate_t, n_past, sel_s):
    blk = lax.broadcasted_iota(jnp.int32, gate_t.shape, 0)
    g = jnp.where(blk < n_past, gate_t, NEG_INF)
    sel_s[...] = g
    rank = jnp.zeros(gate_t.shape, jnp.int32)
    for j in range(n_past):
        gj = jnp.broadcast_to(sel_s[j:j + 1, :], gate_t.shape)
        ahead = (gj > g) | ((gj == g) & (blk > j))
        rank = rank + ahead.astype(jnp.int32)
    sel_s[...] = jnp.where((rank < MOBA_TOPK) & (blk < n_past), 0.0, NEG_INF)


def _fold_rows(a, op):
    return op(a.reshape(a.shape[0] // SUBLANES, SUBLANES, a.shape[1]), axis=0)


MAX_HEADS_PER_STAGE = 8


def _heads_per_stage(n_past, slot_rows):
    per_stage = 1
    while (per_stage < MAX_HEADS_PER_STAGE
           and 2 * per_stage * (n_past + 1) * MOBA_BLOCK <= slot_rows):
        per_stage *= 2
    return per_stage


def _head_scores(hd, slot, lane, n_past, q_s, k_ref, km_ref, sel_s, sc_s, m_s):
    qf = q_s[hd]
    qh = (qf * (HEAD_DIM ** -0.5 * LOG2E)).astype(BF16)
    masked = n_past > MOBA_TOPK
    if masked:
        gate_t = lax.dot_general(km_ref[0, hd].astype(BF16), qf.astype(BF16), NT_DIMS,
                                 preferred_element_type=F32)
        sel_s = sel_s.at[lane]
        _select_blocks(gate_t, n_past, sel_s)
    n_keys = (n_past + 1) * MOBA_BLOCK
    s_all = lax.dot_general(k_ref[0, hd, 0:n_keys, :], qh, NT_DIMS,
                            preferred_element_type=F32)
    m8 = None
    for j in range(n_past + 1):
        rows = slice(j * MOBA_BLOCK, (j + 1) * MOBA_BLOCK)
        s = s_all[rows]
        if j == n_past:
            key = lax.broadcasted_iota(jnp.int32, s.shape, 0)
            qry = lax.broadcasted_iota(jnp.int32, s.shape, 1)
            s = jnp.where(key <= qry, s, NEG_INF)
        elif masked:
            s = s + sel_s[j:j + 1, :]
        sc_s[slot, lane * n_keys + j * MOBA_BLOCK:lane * n_keys + (j + 1) * MOBA_BLOCK] = s
        smax = _fold_rows(s, jnp.max)
        m8 = smax if m8 is None else jnp.maximum(m8, smax)
    m_s[slot, lane] = m8


def _head_output(hd, slot, lane, n_past, vt_ref, sc_s, m_s, o_s):
    n_keys = (n_past + 1) * MOBA_BLOCK
    m = jnp.max(m_s[slot, lane], axis=0, keepdims=True)
    p = jnp.exp2(sc_s[slot, lane * n_keys:(lane + 1) * n_keys] - m)
    denom = jnp.sum(_fold_rows(p, jnp.sum), axis=0, keepdims=True)
    o_t = jnp.dot(vt_ref[0, hd, :, 0:n_keys], p.astype(BF16),
                  preferred_element_type=F32)
    o_s[hd] = o_t * (1.0 / denom)


def _moba_kernel(x_ref, mod_ref, ng_ref, win_ref, k_ref, vt_ref, km_ref, wout_ref, fg_ref,
                 o_ref, q_s, g_s, o_s, sel_s, sc_s, m_s, *, apply_final_norm):
    d = x_ref.shape[2]
    n_heads = k_ref.shape[1]
    n_blk = km_ref.shape[2]
    d_att = n_heads * HEAD_DIM
    qb = pl.program_id(1)
    scores = functools.partial(_head_scores, q_s=q_s, k_ref=k_ref, km_ref=km_ref, sel_s=sel_s,
                               sc_s=sc_s, m_s=m_s)
    output = functools.partial(_head_output, vt_ref=vt_ref, sc_s=sc_s, m_s=m_s, o_s=o_s)

    def block(n_past):
        x = x_ref[0]
        mod = mod_ref[0]
        shift, scale, gate = mod[:, :d], mod[:, d:2 * d], mod[:, 2 * d:]
        hb = _norm_modulate(x, ng_ref[...], shift, scale).astype(BF16)
        q = jnp.dot(hb, win_ref[:, :d_att], preferred_element_type=F32)
        for hd in range(n_heads):
            q_s[hd] = q[:, hd * HEAD_DIM:(hd + 1) * HEAD_DIM]

        per_stage = _heads_per_stage(n_past, sc_s.shape[1])
        n_groups = n_heads // per_stage
        assert sc_s.shape[0] >= min(n_groups, 2)

        def stage(fn, group):
            for lane in range(per_stage):
                fn(group * per_stage + lane, group & 1, lane, n_past)

        stage(scores, 0)
        g_s[...] = jnp.dot(hb, win_ref[:, d_att:], preferred_element_type=F32)

        def step(group, carry):
            stage(output, group - 1)
            stage(scores, group)
            return carry

        lax.fori_loop(1, n_groups, step, 0)
        stage(output, n_groups - 1)

        ys = []
        for hd in range(n_heads):
            gp = g_s[:, hd * HEAD_DIM:(hd + 1) * HEAD_DIM]
            ys.append((o_s[hd].T * (gp * _sigmoid(gp))).astype(BF16))
        early = (n_heads - 1) * HEAD_DIM // MXU_DIM * MXU_DIM
        proj = jnp.dot(jnp.concatenate(ys[:early // HEAD_DIM], axis=1), wout_ref[:early, :],
                       preferred_element_type=F32)
        proj = proj + jnp.dot(jnp.concatenate(ys[early // HEAD_DIM:], axis=1), wout_ref[early:, :],
                              preferred_element_type=F32)
        out = x + gate * proj
        if apply_final_norm:
            ms = jnp.mean(out * out, axis=-1, keepdims=True)
            out = out * lax.rsqrt(ms + EPS) * fg_ref[...]
        o_ref[0] = out

    for n_past in range(n_blk):
        pl.when(qb == n_past)(functools.partial(block, n_past))


def _moba_layer(x, mod, norm_g, w_in, k, v_t, k_mean, w_out, final_g, apply_final_norm):
    bsz, seq, d = x.shape
    n_heads = k.shape[1]
    d_att = n_heads * HEAD_DIM
    n_blk = seq // MOBA_BLOCK
    full = lambda shape: pl.BlockSpec(shape, lambda b, i: (0,) * len(shape))
    per_batch = lambda shape: pl.BlockSpec(shape, lambda b, i: (b,) + (0,) * (len(shape) - 1))
    return pl.pallas_call(
        functools.partial(_moba_kernel, apply_final_norm=apply_final_norm),
        grid=(bsz, n_blk),
        in_specs=[
            pl.BlockSpec((1, MOBA_BLOCK, d), lambda b, i: (b, i, 0)),
            per_batch((1, 1, 3 * d)),
            full((1, d)),
            full((d, 2 * d_att)),
            per_batch((1, n_heads, seq, HEAD_DIM)),
            per_batch((1, n_heads, HEAD_DIM, seq)),
            per_batch((1, n_heads, n_blk, HEAD_DIM)),
            full((d_att, d)),
            full((1, d)),
        ],
        out_specs=pl.BlockSpec((1, MOBA_BLOCK, d), lambda b, i: (b, i, 0)),
        out_shape=jax.ShapeDtypeStruct(x.shape, F32),
        scratch_shapes=[
            pltpu.VMEM((n_heads, MOBA_BLOCK, HEAD_DIM), F32),
            pltpu.VMEM((MOBA_BLOCK, d_att), F32),
            pltpu.VMEM((n_heads, HEAD_DIM, MOBA_BLOCK), F32),
            pltpu.VMEM((MAX_HEADS_PER_STAGE, n_blk, MOBA_BLOCK), F32),
            pltpu.VMEM((1, n_heads * seq, MOBA_BLOCK), F32),
            pltpu.VMEM((2, MAX_HEADS_PER_STAGE, SUBLANES, MOBA_BLOCK), F32),
        ],
        compiler_params=pltpu.CompilerParams(
            dimension_semantics=("arbitrary", "arbitrary"),
            vmem_limit_bytes=VMEM_LIMIT_BYTES),
        name="moba_layer",
    )(x, mod, norm_g, w_in, k, v_t, k_mean, w_out, final_g)


def kernel(x, c, mod_w, mod_b, norm_g, rg_w_in, rg_conv_w, rg_conv_b, rg_w_a, rg_b_a, rg_w_x,
           rg_b_x, rg_lambda, rg_w_out, kv_norm_g, kv_mod_w, kv_mod_b, w_kv, att_w_in,
           att_w_out, final_norm_g):
    bsz, seq, d = x.shape
    depth = mod_w.shape[0]
    n_a = rg_w_in.shape[0]
    n_b = att_w_in.shape[0]
    assert depth == n_a + n_b and seq % MOBA_BLOCK == 0
    d_att = w_kv.shape[1] // 2
    n_heads = d_att // HEAD_DIM
    n_blk = seq // MOBA_BLOCK

    mod = _modulation(c, mod_w, mod_b, tn=768)
    kv_mod = _modulation(c, kv_mod_w[None], kv_mod_b[None], tn=1024)

    row = lambda p: p.reshape(1, -1)
    for i in range(n_a):
        w_ax = jnp.concatenate([rg_w_a[i], rg_w_x[i]], axis=-1).astype(BF16)
        x = _rglru_layer(
            x, mod[i], row(norm_g[i]), rg_w_in[i].astype(BF16), rg_conv_w[i],
            row(rg_conv_b[i]), w_ax, row(rg_b_a[i]), row(rg_b_x[i]), row(rg_lambda[i]),
            rg_w_out[i].astype(BF16), t_len=128)

    k, v_t, k_mean = _shared_kv(x, kv_mod[0][:, None, :], row(kv_norm_g),
                                w_kv[:, :d_att].astype(BF16), w_kv[:, d_att:].T.astype(BF16),
                                tile=2 * MOBA_BLOCK)
    k_mean = k_mean.reshape(bsz, n_blk, n_heads, HEAD_DIM).transpose(0, 2, 1, 3)

    for i in range(n_b):
        layer = n_a + i
        x = _moba_layer(
            x, mod[layer][:, None, :], row(norm_g[layer]), att_w_in[i].astype(BF16), k, v_t,
            k_mean, att_w_out[i].astype(BF16), row(final_norm_g),
            apply_final_norm=(i == n_b - 1))
    return x
```

```python
import functools

import jax
import jax.numpy as jnp
from jax import lax
from jax.experimental import pallas as pl
from jax.experimental.pallas import tpu as pltpu

EPS = 1e-6
RG_C = 8.0
HEAD_DIM = 128
MOBA_BLOCK = 256
MOBA_TOPK = 3
NEG_INF = -1e30
LOG2E = 1.4426950408889634
SUBLANES = 8
MXU_DIM = 256
VMEM_LIMIT_BYTES = 56 * 1024 * 1024

F32 = jnp.float32
BF16 = jnp.bfloat16
NT_DIMS = (((1,), (1,)), ((), ()))


def _sigmoid(z):
    return 1.0 / (1.0 + jnp.exp2(z * (-LOG2E)))


def _sqrt_nonneg(v):
    return jnp.where(v > 0.0, v * lax.rsqrt(v), 0.0)


def _norm_modulate(x, norm_g, shift, scale):
    ms = jnp.mean(x * x, axis=-1, keepdims=True)
    return x * lax.rsqrt(ms + EPS) * (norm_g * (1.0 + scale)) + shift


def _mod_kernel(c_ref, w_ref, b_ref, o_ref):
    c = c_ref[...]
    cs = (c * _sigmoid(c)).astype(BF16)
    w = w_ref[0].astype(BF16)
    o_ref[0] = jnp.dot(cs, w, preferred_element_type=F32) + b_ref[0]


def _modulation(c, w, b, tn):
    n_layers, d, n = w.shape
    bsz = c.shape[0]
    return pl.pallas_call(
        _mod_kernel,
        grid=(n_layers, n // tn),
        in_specs=[
            pl.BlockSpec((bsz, d), lambda l, j: (0, 0)),
            pl.BlockSpec((1, d, tn), lambda l, j: (l, 0, j)),
            pl.BlockSpec((1, 1, tn), lambda l, j: (l, 0, j)),
        ],
        out_specs=pl.BlockSpec((1, bsz, tn), lambda l, j: (l, 0, j)),
        out_shape=jax.ShapeDtypeStruct((n_layers, bsz, n), F32),
        compiler_params=pltpu.CompilerParams(
            dimension_semantics=("arbitrary", "arbitrary"),
            vmem_limit_bytes=VMEM_LIMIT_BYTES),
        name="adaln_mod",
    )(c, w, b.reshape(n_layers, 1, n))


def _tile_copies(hbm_ref, buf, sem, tile, slot, to_hbm):
    t_len, bsz, _ = buf.shape[1:]
    copies = []
    for b in range(bsz):
        hbm = hbm_ref.at[b, pl.ds(tile * t_len, t_len), :]
        vmem = buf.at[slot, :, b, :]
        src, dst = (vmem, hbm) if to_hbm else (hbm, vmem)
        copies.append(pltpu.make_async_copy(src, dst, sem.at[slot]))
    return copies


def _rglru_kernel(x_hbm, mod_ref, ng_ref, win_ref, cw_ref, cb_ref, wax_ref, ba_ref,
                  bx_ref, lam_ref, wout_ref, o_hbm,
                  xbuf, obuf, in_sem, out_sem, hs_s, y_s, utail, hstate):
    _, t_len, bsz, d = xbuf.shape
    rows = t_len * bsz
    n_heads, rb, _ = wax_ref.shape
    conv_w = cw_ref.shape[0]
    i = pl.program_id(0)
    n_tiles = x_hbm.shape[1] // t_len
    slot = i & 1

    @pl.when(i == 0)
    def _():
        for c in _tile_copies(x_hbm, xbuf, in_sem, 0, 0, to_hbm=False):
            c.start()
        utail[...] = jnp.zeros_like(utail)
        hstate[...] = jnp.zeros_like(hstate)

    @pl.when(i + 1 < n_tiles)
    def _():
        for c in _tile_copies(x_hbm, xbuf, in_sem, i + 1, 1 - slot, to_hbm=False):
            c.start()

    for c in _tile_copies(x_hbm, xbuf, in_sem, i, slot, to_hbm=False):
        c.wait()

    x3 = xbuf[slot]
    mod = mod_ref[...]
    shift, scale, gate = mod[:, :d], mod[:, d:2 * d], mod[:, 2 * d:]
    ms = jnp.mean(x3 * x3, axis=-1, keepdims=True)
    h3 = x3 * lax.rsqrt(ms + EPS) * (ng_ref[...] * (1.0 + scale)) + shift
    hb = h3.reshape(rows, d).astype(BF16)
    u3 = jnp.dot(hb, win_ref[:, :d], preferred_element_type=F32).reshape(t_len, bsz, d)

    upad = jnp.concatenate([utail[...], u3], axis=0)
    utail[...] = u3[t_len - (conv_w - 1):]
    uc3 = cb_ref[...] + cw_ref[conv_w - 1:conv_w, :] * u3
    for k in range(conv_w - 1):
        uc3 = uc3 + cw_ref[k:k + 1, :] * upad[k:k + t_len]
    uc = uc3.reshape(rows, d)

    lam = lam_ref[...]
    softplus_neg_lam = jnp.maximum(-lam, 0.0) + jnp.log1p(jnp.exp(-jnp.abs(lam)))
    log2_a_per_r = (-RG_C * LOG2E) * softplus_neg_lam
    for hh in range(n_heads):
        sl = slice(hh * rb, (hh + 1) * rb)
        uch = uc[:, sl]
        z = jnp.dot(uch.astype(BF16), wax_ref[hh], preferred_element_type=F32)
        r = _sigmoid(z[:, :rb] + ba_ref[:, sl])
        gi = _sigmoid(z[:, rb:] + bx_ref[:, sl])
        a = jnp.exp2(r * log2_a_per_r[:, sl])
        b_in = _sqrt_nonneg(1.0 - a * a) * (gi * uch)
        a3 = a.reshape(t_len, bsz, rb)
        b3 = b_in.reshape(t_len, bsz, rb)
        h_run = hstate[:, sl]
        for t in range(t_len):
            h_run = a3[t] * h_run + b3[t]
            hs_s[t, :, sl] = h_run
        hstate[:, sl] = h_run
        gpath = jnp.dot(hb, win_ref[:, d + hh * rb:d + (hh + 1) * rb], preferred_element_type=F32)
        y = hs_s[:, :, sl].reshape(rows, rb) * (gpath * _sigmoid(gpath))
        y_s[:, sl] = y.astype(BF16)

    proj = jnp.dot(y_s[...], wout_ref[...], preferred_element_type=F32)
    out3 = x3 + gate * proj.reshape(t_len, bsz, d)

    @pl.when(i >= 2)
    def _():
        for c in _tile_copies(o_hbm, obuf, out_sem, i - 2, slot, to_hbm=True):
            c.wait()

    obuf[slot] = out3
    for c in _tile_copies(o_hbm, obuf, out_sem, i, slot, to_hbm=True):
        c.start()

    @pl.when(i == n_tiles - 1)
    def _():
        if n_tiles >= 2:
            for c in _tile_copies(o_hbm, obuf, out_sem, i - 1, 1 - slot, to_hbm=True):
                c.wait()
        for c in _tile_copies(o_hbm, obuf, out_sem, i, slot, to_hbm=True):
            c.wait()


def _rglru_layer(x, mod, norm_g, w_in, conv_w, conv_b, w_ax, b_a, b_x, lam, w_out, t_len):
    bsz, seq, d = x.shape
    assert bsz == SUBLANES and seq % t_len == 0 and t_len >= conv_w.shape[0] - 1
    n_heads, rb, _ = w_ax.shape
    full = lambda shape: pl.BlockSpec(shape, lambda i: (0,) * len(shape))
    return pl.pallas_call(
        _rglru_kernel,
        grid=(seq // t_len,),
        in_specs=[
            pl.BlockSpec(memory_space=pl.ANY),
            full((bsz, 3 * d)),
            full((1, d)),
            full((d, 2 * d)),
            full(conv_w.shape),
            full((1, d)),
            full((n_heads, rb, 2 * rb)),
            full((1, d)),
            full((1, d)),
            full((1, d)),
            full((d, d)),
        ],
        out_specs=pl.BlockSpec(memory_space=pl.ANY),
        out_shape=jax.ShapeDtypeStruct(x.shape, F32),
        scratch_shapes=[
            pltpu.VMEM((2, t_len, bsz, d), F32),
            pltpu.VMEM((2, t_len, bsz, d), F32),
            pltpu.SemaphoreType.DMA((2,)),
            pltpu.SemaphoreType.DMA((2,)),
            pltpu.VMEM((t_len, bsz, d), F32),
            pltpu.VMEM((t_len * bsz, d), BF16),
            pltpu.VMEM((conv_w.shape[0] - 1, bsz, d), F32),
            pltpu.VMEM((bsz, d), F32),
        ],
        compiler_params=pltpu.CompilerParams(
            dimension_semantics=("arbitrary",),
            vmem_limit_bytes=VMEM_LIMIT_BYTES),
        name="rglru_layer",
    )(x, mod, norm_g, w_in, conv_w, conv_b, w_ax, b_a, b_x, lam, w_out)


def _kv_kernel(x_ref, mod_ref, ng_ref, wk_ref, wvt_ref, k_ref, vt_ref, km_ref):
    d = x_ref.shape[2]
    n_heads = k_ref.shape[1]
    mod = mod_ref[0]
    h = _norm_modulate(x_ref[0], ng_ref[...], mod[:, :d], mod[:, d:]).astype(BF16)
    k = jnp.dot(h, wk_ref[...], preferred_element_type=F32)
    vt = lax.dot_general(wvt_ref[...], h, NT_DIMS, preferred_element_type=F32)
    for j in range(km_ref.shape[1]):
        km_ref[0, j] = jnp.mean(k[j * MOBA_BLOCK:(j + 1) * MOBA_BLOCK], axis=0, keepdims=True)
    for hd in range(n_heads):
        sl = slice(hd * HEAD_DIM, (hd + 1) * HEAD_DIM)
        k_ref[0, hd] = k[:, sl].astype(BF16)
        vt_ref[0, hd] = vt[sl, :].astype(BF16)


def _shared_kv(x, mod, norm_g, w_k, w_v_t, tile):
    bsz, seq, d = x.shape
    d_att = w_k.shape[1]
    n_heads = d_att // HEAD_DIM
    n_blk = seq // MOBA_BLOCK
    blk_per_tile = tile // MOBA_BLOCK
    assert tile % MOBA_BLOCK == 0 and seq % tile == 0
    return pl.pallas_call(
        _kv_kernel,
        grid=(bsz, seq // tile),
        in_specs=[
            pl.BlockSpec((1, tile, d), lambda b, i: (b, i, 0)),
            pl.BlockSpec((1, 1, 2 * d), lambda b, i: (b, 0, 0)),
            pl.BlockSpec((1, d), lambda b, i: (0, 0)),
            pl.BlockSpec((d, d_att), lambda b, i: (0, 0)),
            pl.BlockSpec((d_att, d), lambda b, i: (0, 0)),
        ],
        out_specs=[
            pl.BlockSpec((1, n_heads, tile, HEAD_DIM), lambda b, i: (b, 0, i, 0)),
            pl.BlockSpec((1, n_heads, HEAD_DIM, tile), lambda b, i: (b, 0, 0, i)),
            pl.BlockSpec((1, blk_per_tile, 1, d_att), lambda b, i: (b, i, 0, 0)),
        ],
        out_shape=[jax.ShapeDtypeStruct((bsz, n_heads, seq, HEAD_DIM), BF16),
                   jax.ShapeDtypeStruct((bsz, n_heads, HEAD_DIM, seq), BF16),
                   jax.ShapeDtypeStruct((bsz, n_blk, 1, d_att), F32)],
        compiler_params=pltpu.CompilerParams(
            dimension_semantics=("arbitrary", "arbitrary"),
            vmem_limit_bytes=VMEM_LIMIT_BYTES),
        name="shared_kv",
    )(x, mod, norm_g, w_k, w_v_t)


def _select_blocks(gate_t, n_past, sel_s):
    blk = lax.broadcasted_iota(jnp.int32, gate_t.shape, 0)
    g = jnp.where(blk < n_past, gate_t, NEG_INF)
    sel_s[...] = g
    rank = jnp.zeros(gate_t.shape, jnp.int32)
    for j in range(n_past):
        gj = jnp.broadcast_to(sel_s[j:j + 1, :], gate_t.shape)
        ahead = (gj > g) | ((gj == g) & (blk > j))
        rank = rank + ahead.astype(jnp.int32)
    sel_s[...] = jnp.where((rank < MOBA_TOPK) & (blk < n_past), 0.0, NEG_INF)


def _fold_rows(a, op):
    return op(a.reshape(a.shape[0] // SUBLANES, SUBLANES, a.shape[1]), axis=0)


MAX_HEADS_PER_STAGE = 8


def _heads_per_stage(n_past, slot_rows):
    per_stage = 1
    while (per_stage < MAX_HEADS_PER_STAGE
           and 2 * per_stage * (n_past + 1) * MOBA_BLOCK <= slot_rows):
        per_stage *= 2
    return per_stage


def _head_scores(hd, slot, lane, n_past, q_s, k_ref, km_ref, sel_s, sc_s, m_s):
    qf = q_s[hd]
    qh = (qf * (HEAD_DIM ** -0.5 * LOG2E)).astype(BF16)
    masked = n_past > MOBA_TOPK
    if masked:
        gate_t = lax.dot_general(km_ref[0, hd].astype(BF16), qf.astype(BF16), NT_DIMS,
                                 preferred_element_type=F32)
        sel_s = sel_s.at[lane]
        _select_blocks(gate_t, n_past, sel_s)
    n_keys = (n_past + 1) * MOBA_BLOCK
    s_all = lax.dot_general(k_ref[0, hd, 0:n_keys, :], qh, NT_DIMS,
                            preferred_element_type=F32)
    m8 = None
    for j in range(n_past + 1):
        rows = slice(j * MOBA_BLOCK, (j + 1) * MOBA_BLOCK)
        s = s_all[rows]
        if j == n_past:
            key = lax.broadcasted_iota(jnp.int32, s.shape, 0)
            qry = lax.broadcasted_iota(jnp.int32, s.shape, 1)
            s = jnp.where(key <= qry, s, NEG_INF)
        elif masked:
            s = s + sel_s[j:j + 1, :]
        sc_s[slot, lane * n_keys + j * MOBA_BLOCK:lane * n_keys + (j + 1) * MOBA_BLOCK] = s
        smax = _fold_rows(s, jnp.max)
        m8 = smax if m8 is None else jnp.maximum(m8, smax)
    m_s[slot, lane] = m8


def _head_output(hd, slot, lane, n_past, vt_ref, sc_s, m_s, o_s):
    n_keys = (n_past + 1) * MOBA_BLOCK
    m = jnp.max(m_s[slot, lane], axis=0, keepdims=True)
    p = jnp.exp2(sc_s[slot, lane * n_keys:(lane + 1) * n_keys] - m)
    denom = jnp.sum(_fold_rows(p, jnp.sum), axis=0, keepdims=True)
    o_t = jnp.dot(vt_ref[0, hd, :, 0:n_keys], p.astype(BF16),
                  preferred_element_type=F32)
    o_s[hd] = o_t * (1.0 / denom)


def _moba_kernel(x_ref, mod_ref, ng_ref, win_ref, k_ref, vt_ref, km_ref, wout_ref, fg_ref,
                 o_ref, x_s, q_s, g_s, o_s, sel_s, sc_s, m_s):
    d = x_ref.shape[2]
    n_layers = win_ref.shape[0]
    n_heads = k_ref.shape[1]
    n_blk = km_ref.shape[2]
    d_att = n_heads * HEAD_DIM
    qb = pl.program_id(1)
    scores = functools.partial(_head_scores, q_s=q_s, k_ref=k_ref, km_ref=km_ref, sel_s=sel_s,
                               sc_s=sc_s, m_s=m_s)
    output = functools.partial(_head_output, vt_ref=vt_ref, sc_s=sc_s, m_s=m_s, o_s=o_s)

    def layer(n_past, lyr):
        x = x_s[...]
        mod = mod_ref[0, pl.ds(lyr, 1), :]
        shift, scale, gate = mod[:, :d], mod[:, d:2 * d], mod[:, 2 * d:]
        hb = _norm_modulate(x, ng_ref[lyr], shift, scale).astype(BF16)
        q = jnp.dot(hb, win_ref[lyr, :, :d_att], preferred_element_type=F32)
        for hd in range(n_heads):
            q_s[hd] = q[:, hd * HEAD_DIM:(hd + 1) * HEAD_DIM]

        per_stage = _heads_per_stage(n_past, sc_s.shape[1])
        n_groups = n_heads // per_stage
        assert sc_s.shape[0] >= min(n_groups, 2)

        def stage(fn, group):
            for lane in range(per_stage):
                fn(group * per_stage + lane, group & 1, lane, n_past)

        stage(scores, 0)
        g_s[...] = jnp.dot(hb, win_ref[lyr, :, d_att:], preferred_element_type=F32)

        def step(group, carry):
            stage(output, group - 1)
            stage(scores, group)
            return carry

        lax.fori_loop(1, n_groups, step, 0)
        stage(output, n_groups - 1)

        ys = []
        for hd in range(n_heads):
            gp = g_s[:, hd * HEAD_DIM:(hd + 1) * HEAD_DIM]
            ys.append((o_s[hd].T * (gp * _sigmoid(gp))).astype(BF16))
        early = (n_heads - 1) * HEAD_DIM // MXU_DIM * MXU_DIM
        proj = jnp.dot(jnp.concatenate(ys[:early // HEAD_DIM], axis=1), wout_ref[lyr, :early, :],
                       preferred_element_type=F32)
        proj = proj + jnp.dot(jnp.concatenate(ys[early // HEAD_DIM:], axis=1),
                              wout_ref[lyr, early:, :], preferred_element_type=F32)
        x_s[...] = x + gate * proj

    def block(n_past):
        x_s[...] = x_ref[0]

        def one_layer(lyr, carry):
            layer(n_past, lyr)
            return carry

        lax.fori_loop(0, n_layers, one_layer, 0)
        out = x_s[...]
        ms = jnp.mean(out * out, axis=-1, keepdims=True)
        o_ref[0] = out * lax.rsqrt(ms + EPS) * fg_ref[...]

    for n_past in range(n_blk):
        pl.when(qb == n_past)(functools.partial(block, n_past))


def _moba_stack(x, mod, norm_g, w_in, k, v_t, k_mean, w_out, final_g):
    bsz, seq, d = x.shape
    n_layers = w_in.shape[0]
    n_heads = k.shape[1]
    d_att = n_heads * HEAD_DIM
    n_blk = seq // MOBA_BLOCK
    full = lambda shape: pl.BlockSpec(shape, lambda b, i: (0,) * len(shape))
    per_batch = lambda shape, **kw: pl.BlockSpec(
        shape, lambda b, i: (b,) + (0,) * (len(shape) - 1), **kw)
    return pl.pallas_call(
        _moba_kernel,
        grid=(bsz, n_blk),
        in_specs=[
            pl.BlockSpec((1, MOBA_BLOCK, d), lambda b, i: (b, i, 0)),
            per_batch((1, n_layers, 3 * d)),
            full((n_layers, 1, d)),
            full((n_layers, d, 2 * d_att)),
            per_batch((1, n_heads, seq, HEAD_DIM), pipeline_mode=pl.Buffered(1)),
            per_batch((1, n_heads, HEAD_DIM, seq), pipeline_mode=pl.Buffered(1)),
            per_batch((1, n_heads, n_blk, HEAD_DIM)),
            full((n_layers, d_att, d)),
            full((1, d)),
        ],
        out_specs=pl.BlockSpec((1, MOBA_BLOCK, d), lambda b, i: (b, i, 0)),
        out_shape=jax.ShapeDtypeStruct(x.shape, F32),
        scratch_shapes=[
            pltpu.VMEM((MOBA_BLOCK, d), F32),
            pltpu.VMEM((n_heads, MOBA_BLOCK, HEAD_DIM), F32),
            pltpu.VMEM((MOBA_BLOCK, d_att), F32),
            pltpu.VMEM((n_heads, HEAD_DIM, MOBA_BLOCK), F32),
            pltpu.VMEM((MAX_HEADS_PER_STAGE, n_blk, MOBA_BLOCK), F32),
            pltpu.VMEM((1, n_heads * seq, MOBA_BLOCK), F32),
            pltpu.VMEM((2, MAX_HEADS_PER_STAGE, SUBLANES, MOBA_BLOCK), F32),
        ],
        compiler_params=pltpu.CompilerParams(
            dimension_semantics=("arbitrary", "arbitrary"),
            vmem_limit_bytes=VMEM_LIMIT_BYTES),
        name="moba_stack",
    )(x, mod, norm_g, w_in, k, v_t, k_mean, w_out, final_g)


def kernel(x, c, mod_w, mod_b, norm_g, rg_w_in, rg_conv_w, rg_conv_b, rg_w_a, rg_b_a, rg_w_x,
           rg_b_x, rg_lambda, rg_w_out, kv_norm_g, kv_mod_w, kv_mod_b, w_kv, att_w_in,
           att_w_out, final_norm_g):
    bsz, seq, d = x.shape
    depth = mod_w.shape[0]
    n_a = rg_w_in.shape[0]
    n_b = att_w_in.shape[0]
    assert depth == n_a + n_b and seq % MOBA_BLOCK == 0
    d_att = w_kv.shape[1] // 2
    n_heads = d_att // HEAD_DIM
    n_blk = seq // MOBA_BLOCK

    mod = _modulation(c, mod_w, mod_b, tn=768)
    kv_mod = _modulation(c, kv_mod_w[None], kv_mod_b[None], tn=1024)

    row = lambda p: p.reshape(1, -1)
    for i in range(n_a):
        w_ax = jnp.concatenate([rg_w_a[i], rg_w_x[i]], axis=-1).astype(BF16)
        x = _rglru_layer(
            x, mod[i], row(norm_g[i]), rg_w_in[i].astype(BF16), rg_conv_w[i],
            row(rg_conv_b[i]), w_ax, row(rg_b_a[i]), row(rg_b_x[i]), row(rg_lambda[i]),
            rg_w_out[i].astype(BF16), t_len=128)

    k, v_t, k_mean = _shared_kv(x, kv_mod[0][:, None, :], row(kv_norm_g),
                                w_kv[:, :d_att].astype(BF16), w_kv[:, d_att:].T.astype(BF16),
                                tile=2 * MOBA_BLOCK)
    k_mean = k_mean.reshape(bsz, n_blk, n_heads, HEAD_DIM).transpose(0, 2, 1, 3)

    return _moba_stack(
        x, mod[n_a:].transpose(1, 0, 2), norm_g[n_a:, None, :], att_w_in.astype(BF16), k, v_t,
        k_mean, att_w_out.astype(BF16), row(final_norm_g))
```

```python
import functools

import jax
import jax.numpy as jnp
from jax import lax
from jax.experimental import pallas as pl
from jax.experimental.pallas import tpu as pltpu

EPS = 1e-6
RG_C = 8.0
HEAD_DIM = 128
MOBA_BLOCK = 256
MOBA_TOPK = 3
NEG_INF = -1e30
LOG2E = 1.4426950408889634
SUBLANES = 8
MXU_DIM = 256
VMEM_LIMIT_BYTES = 56 * 1024 * 1024

F32 = jnp.float32
BF16 = jnp.bfloat16
NT_DIMS = (((1,), (1,)), ((), ()))


def _sigmoid(z):
    return 1.0 / (1.0 + jnp.exp2(z * (-LOG2E)))


def _sqrt_nonneg(v):
    return jnp.where(v > 0.0, v * lax.rsqrt(v), 0.0)


def _norm_modulate(x, norm_g, shift, scale):
    ms = jnp.mean(x * x, axis=-1, keepdims=True)
    return x * lax.rsqrt(ms + EPS) * (norm_g * (1.0 + scale)) + shift


def _mod_kernel(c_ref, w_ref, b_ref, o_ref):
    c = c_ref[...]
    cs = (c * _sigmoid(c)).astype(BF16)
    w = w_ref[0].astype(BF16)
    o_ref[0] = jnp.dot(cs, w, preferred_element_type=F32) + b_ref[0]


def _modulation(c, w, b, tn):
    n_layers, d, n = w.shape
    bsz = c.shape[0]
    return pl.pallas_call(
        _mod_kernel,
        grid=(n_layers, n // tn),
        in_specs=[
            pl.BlockSpec((bsz, d), lambda l, j: (0, 0)),
            pl.BlockSpec((1, d, tn), lambda l, j: (l, 0, j)),
            pl.BlockSpec((1, 1, tn), lambda l, j: (l, 0, j)),
        ],
        out_specs=pl.BlockSpec((1, bsz, tn), lambda l, j: (l, 0, j)),
        out_shape=jax.ShapeDtypeStruct((n_layers, bsz, n), F32),
        compiler_params=pltpu.CompilerParams(
            dimension_semantics=("arbitrary", "arbitrary"),
            vmem_limit_bytes=VMEM_LIMIT_BYTES),
        name="adaln_mod",
    )(c, w, b.reshape(n_layers, 1, n))


def _tile_copies(hbm_ref, buf, sem, tile, slot, to_hbm):
    t_len, bsz, _ = buf.shape[1:]
    copies = []
    for b in range(bsz):
        hbm = hbm_ref.at[b, pl.ds(tile * t_len, t_len), :]
        vmem = buf.at[slot, :, b, :]
        src, dst = (vmem, hbm) if to_hbm else (hbm, vmem)
        copies.append(pltpu.make_async_copy(src, dst, sem.at[slot]))
    return copies


def _rglru_kernel(x_hbm, mod_ref, ng_ref, win_ref, cw_ref, cb_ref, wax_ref, ba_ref,
                  bx_ref, lam_ref, wout_ref, o_hbm,
                  xbuf, obuf, in_sem, out_sem, hs_s, y_s, utail, hstate):
    _, t_len, bsz, d = xbuf.shape
    rows = t_len * bsz
    n_heads, rb, _ = wax_ref.shape
    conv_w = cw_ref.shape[0]
    i = pl.program_id(0)
    n_tiles = x_hbm.shape[1] // t_len
    slot = i & 1

    @pl.when(i == 0)
    def _():
        for c in _tile_copies(x_hbm, xbuf, in_sem, 0, 0, to_hbm=False):
            c.start()
        utail[...] = jnp.zeros_like(utail)
        hstate[...] = jnp.zeros_like(hstate)

    @pl.when(i + 1 < n_tiles)
    def _():
        for c in _tile_copies(x_hbm, xbuf, in_sem, i + 1, 1 - slot, to_hbm=False):
            c.start()

    for c in _tile_copies(x_hbm, xbuf, in_sem, i, slot, to_hbm=False):
        c.wait()

    x3 = xbuf[slot]
    mod = mod_ref[...]
    shift, scale, gate = mod[:, :d], mod[:, d:2 * d], mod[:, 2 * d:]
    ms = jnp.mean(x3 * x3, axis=-1, keepdims=True)
    h3 = x3 * lax.rsqrt(ms + EPS) * (ng_ref[...] * (1.0 + scale)) + shift
    hb = h3.reshape(rows, d).astype(BF16)
    u3 = jnp.dot(hb, win_ref[:, :d], preferred_element_type=F32).reshape(t_len, bsz, d)

    upad = jnp.concatenate([utail[...], u3], axis=0)
    utail[...] = u3[t_len - (conv_w - 1):]
    uc3 = cb_ref[...] + cw_ref[conv_w - 1:conv_w, :] * u3
    for k in range(conv_w - 1):
        uc3 = uc3 + cw_ref[k:k + 1, :] * upad[k:k + t_len]
    uc = uc3.reshape(rows, d)

    lam = lam_ref[...]
    softplus_neg_lam = jnp.maximum(-lam, 0.0) + jnp.log1p(jnp.exp(-jnp.abs(lam)))
    log2_a_per_r = (-RG_C * LOG2E) * softplus_neg_lam
    for hh in range(n_heads):
        sl = slice(hh * rb, (hh + 1) * rb)
        uch = uc[:, sl]
        z = jnp.dot(uch.astype(BF16), wax_ref[hh], preferred_element_type=F32)
        r = _sigmoid(z[:, :rb] + ba_ref[:, sl])
        gi = _sigmoid(z[:, rb:] + bx_ref[:, sl])
        a = jnp.exp2(r * log2_a_per_r[:, sl])
        b_in = _sqrt_nonneg(1.0 - a * a) * (gi * uch)
        a3 = a.reshape(t_len, bsz, rb)
        b3 = b_in.reshape(t_len, bsz, rb)
        h_run = hstate[:, sl]
        for t in range(t_len):
            h_run = a3[t] * h_run + b3[t]
            hs_s[t, :, sl] = h_run
        hstate[:, sl] = h_run
        gpath = jnp.dot(hb, win_ref[:, d + hh * rb:d + (hh + 1) * rb], preferred_element_type=F32)
        y = hs_s[:, :, sl].reshape(rows, rb) * (gpath * _sigmoid(gpath))
        y_s[:, sl] = y.astype(BF16)

    proj = jnp.dot(y_s[...], wout_ref[...], preferred_element_type=F32)
    out3 = x3 + gate * proj.reshape(t_len, bsz, d)

    @pl.when(i >= 2)
    def _():
        for c in _tile_copies(o_hbm, obuf, out_sem, i - 2, slot, to_hbm=True):
            c.wait()

    obuf[slot] = out3
    for c in _tile_copies(o_hbm, obuf, out_sem, i, slot, to_hbm=True):
        c.start()

    @pl.when(i == n_tiles - 1)
    def _():
        if n_tiles >= 2:
            for c in _tile_copies(o_hbm, obuf, out_sem, i - 1, 1 - slot, to_hbm=True):
                c.wait()
        for c in _tile_copies(o_hbm, obuf, out_sem, i, slot, to_hbm=True):
            c.wait()


def _rglru_layer(x, mod, norm_g, w_in, conv_w, conv_b, w_ax, b_a, b_x, lam, w_out, t_len):
    bsz, seq, d = x.shape
    assert bsz == SUBLANES and seq % t_len == 0 and t_len >= conv_w.shape[0] - 1
    n_heads, rb, _ = w_ax.shape
    full = lambda shape: pl.BlockSpec(shape, lambda i: (0,) * len(shape))
    return pl.pallas_call(
        _rglru_kernel,
        grid=(seq // t_len,),
        in_specs=[
            pl.BlockSpec(memory_space=pl.ANY),
            full((bsz, 3 * d)),
            full((1, d)),
            full((d, 2 * d)),
            full(conv_w.shape),
            full((1, d)),
            full((n_heads, rb, 2 * rb)),
            full((1, d)),
            full((1, d)),
            full((1, d)),
            full((d, d)),
        ],
        out_specs=pl.BlockSpec(memory_space=pl.ANY),
        out_shape=jax.ShapeDtypeStruct(x.shape, F32),
        scratch_shapes=[
            pltpu.VMEM((2, t_len, bsz, d), F32),
            pltpu.VMEM((2, t_len, bsz, d), F32),
            pltpu.SemaphoreType.DMA((2,)),
            pltpu.SemaphoreType.DMA((2,)),
            pltpu.VMEM((t_len, bsz, d), F32),
            pltpu.VMEM((t_len * bsz, d), BF16),
            pltpu.VMEM((conv_w.shape[0] - 1, bsz, d), F32),
            pltpu.VMEM((bsz, d), F32),
        ],
        compiler_params=pltpu.CompilerParams(
            dimension_semantics=("arbitrary",),
            vmem_limit_bytes=VMEM_LIMIT_BYTES),
        name="rglru_layer",
    )(x, mod, norm_g, w_in, conv_w, conv_b, w_ax, b_a, b_x, lam, w_out)


def _kv_kernel(x_ref, mod_ref, ng_ref, wk_ref, wvt_ref, k_ref, vt_ref, km_ref):
    d = x_ref.shape[2]
    n_heads = k_ref.shape[1]
    mod = mod_ref[0]
    h = _norm_modulate(x_ref[0], ng_ref[...], mod[:, :d], mod[:, d:]).astype(BF16)
    k = jnp.dot(h, wk_ref[...], preferred_element_type=F32)
    vt = lax.dot_general(wvt_ref[...], h, NT_DIMS, preferred_element_type=F32)
    for j in range(km_ref.shape[1]):
        km_ref[0, j] = jnp.mean(k[j * MOBA_BLOCK:(j + 1) * MOBA_BLOCK], axis=0, keepdims=True)
    for hd in range(n_heads):
        sl = slice(hd * HEAD_DIM, (hd + 1) * HEAD_DIM)
        k_ref[0, hd] = k[:, sl].astype(BF16)
        vt_ref[0, hd] = vt[sl, :].astype(BF16)


def _shared_kv(x, mod, norm_g, w_k, w_v_t, tile):
    bsz, seq, d = x.shape
    d_att = w_k.shape[1]
    n_heads = d_att // HEAD_DIM
    n_blk = seq // MOBA_BLOCK
    blk_per_tile = tile // MOBA_BLOCK
    assert tile % MOBA_BLOCK == 0 and seq % tile == 0
    return pl.pallas_call(
        _kv_kernel,
        grid=(bsz, seq // tile),
        in_specs=[
            pl.BlockSpec((1, tile, d), lambda b, i: (b, i, 0)),
            pl.BlockSpec((1, 1, 2 * d), lambda b, i: (b, 0, 0)),
            pl.BlockSpec((1, d), lambda b, i: (0, 0)),
            pl.BlockSpec((d, d_att), lambda b, i: (0, 0)),
            pl.BlockSpec((d_att, d), lambda b, i: (0, 0)),
        ],
        out_specs=[
            pl.BlockSpec((1, n_heads, tile, HEAD_DIM), lambda b, i: (b, 0, i, 0)),
            pl.BlockSpec((1, n_heads, HEAD_DIM, tile), lambda b, i: (b, 0, 0, i)),
            pl.BlockSpec((1, blk_per_tile, 1, d_att), lambda b, i: (b, i, 0, 0)),
        ],
        out_shape=[jax.ShapeDtypeStruct((bsz, n_heads, seq, HEAD_DIM), BF16),
                   jax.ShapeDtypeStruct((bsz, n_heads, HEAD_DIM, seq), BF16),
                   jax.ShapeDtypeStruct((bsz, n_blk, 1, d_att), F32)],
        compiler_params=pltpu.CompilerParams(
            dimension_semantics=("arbitrary", "arbitrary"),
            vmem_limit_bytes=VMEM_LIMIT_BYTES),
        name="shared_kv",
    )(x, mod, norm_g, w_k, w_v_t)


def _select_blocks(gate_t, n_past, sel_s):
    blk = lax.broadcasted_iota(jnp.int32, gate_t.shape, 0)
    g = jnp.where(blk < n_past, gate_t, NEG_INF)
    sel_s[...] = g
    rank = jnp.zeros(gate_t.shape, jnp.int32)
    for j in range(n_past):
        gj = jnp.broadcast_to(sel_s[j:j + 1, :], gate_t.shape)
        ahead = (gj > g) | ((gj == g) & (blk > j))
        rank = rank + ahead.astype(jnp.int32)
    sel_s[...] = jnp.where((rank < MOBA_TOPK) & (blk < n_past), 0.0, NEG_INF)


def _fold_rows(a, op):
    return op(a.reshape(a.shape[0] // SUBLANES, SUBLANES, a.shape[1]), axis=0)


MAX_HEADS_PER_STAGE = 8


def _heads_per_stage(n_past, slot_rows):
    per_stage = 1
    while (per_stage < MAX_HEADS_PER_STAGE
           and 2 * per_stage * (n_past + 1) * MOBA_BLOCK <= slot_rows):
        per_stage *= 2
    return per_stage


def _head_scores(hd, slot, lane, n_past, q_s, k_ref, km_ref, sel_s, sc_s, m_s):
    qf = q_s[hd]
    qh = (qf * (HEAD_DIM ** -0.5 * LOG2E)).astype(BF16)
    masked = n_past > MOBA_TOPK
    if masked:
        gate_t = lax.dot_general(km_ref[0, hd].astype(BF16), qf.astype(BF16), NT_DIMS,
                                 preferred_element_type=F32)
        sel_s = sel_s.at[lane]
        _select_blocks(gate_t, n_past, sel_s)
    n_keys = (n_past + 1) * MOBA_BLOCK
    s_all = lax.dot_general(k_ref[0, hd, 0:n_keys, :], qh, NT_DIMS,
                            preferred_element_type=F32)
    m8 = None
    for j in range(n_past + 1):
        rows = slice(j * MOBA_BLOCK, (j + 1) * MOBA_BLOCK)
        s = s_all[rows]
        if j == n_past:
            key = lax.broadcasted_iota(jnp.int32, s.shape, 0)
            qry = lax.broadcasted_iota(jnp.int32, s.shape, 1)
            s = jnp.where(key <= qry, s, NEG_INF)
        elif masked:
            s = s + sel_s[j:j + 1, :]
        sc_s[slot, lane * n_keys + j * MOBA_BLOCK:lane * n_keys + (j + 1) * MOBA_BLOCK] = s
        smax = _fold_rows(s, jnp.max)
        m8 = smax if m8 is None else jnp.maximum(m8, smax)
    m_s[slot, lane] = m8


def _head_output(hd, slot, lane, n_past, vt_ref, sc_s, m_s, o_s):
    n_keys = (n_past + 1) * MOBA_BLOCK
    m = jnp.max(m_s[slot, lane], axis=0, keepdims=True)
    p = jnp.exp2(sc_s[slot, lane * n_keys:(lane + 1) * n_keys] - m)
    denom = jnp.sum(_fold_rows(p, jnp.sum), axis=0, keepdims=True)
    o_t = jnp.dot(vt_ref[0, hd, :, 0:n_keys], p.astype(BF16),
                  preferred_element_type=F32)
    o_s[hd] = o_t * (1.0 / denom)


def _moba_kernel(x_ref, mod_ref, ng_ref, win_ref, k_ref, vt_ref, km_ref, wout_ref, fg_ref,
                 o_ref, x_s, hb_s, q_s, g_s, o_s, sel_s, sc_s, m_s):
    d = x_ref.shape[2]
    n_layers = win_ref.shape[0]
    n_heads = k_ref.shape[1]
    n_blk = km_ref.shape[2]
    d_att = n_heads * HEAD_DIM
    qb = pl.program_id(1)
    scores = functools.partial(_head_scores, q_s=q_s, k_ref=k_ref, km_ref=km_ref, sel_s=sel_s,
                               sc_s=sc_s, m_s=m_s)
    output = functools.partial(_head_output, vt_ref=vt_ref, sc_s=sc_s, m_s=m_s, o_s=o_s)

    def project_q(lyr):
        mod = mod_ref[0, pl.ds(lyr, 1), :]
        hb = _norm_modulate(x_s[...], ng_ref[lyr], mod[:, :d], mod[:, d:2 * d]).astype(BF16)
        hb_s[...] = hb
        q = jnp.dot(hb, win_ref[lyr, :, :d_att], preferred_element_type=F32)
        for hd in range(n_heads):
            q_s[hd] = q[:, hd * HEAD_DIM:(hd + 1) * HEAD_DIM]

    def attend_and_project(n_past, lyr):
        per_stage = _heads_per_stage(n_past, sc_s.shape[1])
        n_groups = n_heads // per_stage
        assert sc_s.shape[0] >= min(n_groups, 2)

        def stage(fn, group):
            for lane in range(per_stage):
                fn(group * per_stage + lane, group & 1, lane, n_past)

        stage(scores, 0)
        g_s[...] = jnp.dot(hb_s[...], win_ref[lyr, :, d_att:], preferred_element_type=F32)

        def step(group, carry):
            stage(output, group - 1)
            stage(scores, group)
            return carry

        lax.fori_loop(1, n_groups, step, 0)
        stage(output, n_groups - 1)

        ys = []
        for hd in range(n_heads):
            gp = g_s[:, hd * HEAD_DIM:(hd + 1) * HEAD_DIM]
            ys.append((o_s[hd].T * (gp * _sigmoid(gp))).astype(BF16))
        early = (n_heads - 1) * HEAD_DIM // MXU_DIM * MXU_DIM
        proj = jnp.dot(jnp.concatenate(ys[:early // HEAD_DIM], axis=1), wout_ref[lyr, :early, :],
                       preferred_element_type=F32)
        proj = proj + jnp.dot(jnp.concatenate(ys[early // HEAD_DIM:], axis=1),
                              wout_ref[lyr, early:, :], preferred_element_type=F32)
        gate = mod_ref[0, pl.ds(lyr, 1), 2 * d:]
        x_s[...] = x_s[...] + gate * proj

    x_s[...] = x_ref[0]

    def one_layer(lyr, carry):
        project_q(lyr)
        for n_past in range(n_blk):
            pl.when(qb == n_past)(functools.partial(attend_and_project, n_past, lyr))
        return carry

    lax.fori_loop(0, n_layers, one_layer, 0)
    out = x_s[...]
    ms = jnp.mean(out * out, axis=-1, keepdims=True)
    o_ref[0] = out * lax.rsqrt(ms + EPS) * fg_ref[...]


def _moba_stack(x, mod, norm_g, w_in, k, v_t, k_mean, w_out, final_g):
    bsz, seq, d = x.shape
    n_layers = w_in.shape[0]
    n_heads = k.shape[1]
    d_att = n_heads * HEAD_DIM
    n_blk = seq // MOBA_BLOCK
    full = lambda shape: pl.BlockSpec(shape, lambda b, i: (0,) * len(shape))
    per_batch = lambda shape, **kw: pl.BlockSpec(
        shape, lambda b, i: (b,) + (0,) * (len(shape) - 1), **kw)
    return pl.pallas_call(
        _moba_kernel,
        grid=(bsz, n_blk),
        in_specs=[
            pl.BlockSpec((1, MOBA_BLOCK, d), lambda b, i: (b, i, 0)),
            per_batch((1, n_layers, 3 * d)),
            full((n_layers, 1, d)),
            full((n_layers, d, 2 * d_att)),
            per_batch((1, n_heads, seq, HEAD_DIM), pipeline_mode=pl.Buffered(1)),
            per_batch((1, n_heads, HEAD_DIM, seq), pipeline_mode=pl.Buffered(1)),
            per_batch((1, n_heads, n_blk, HEAD_DIM)),
            full((n_layers, d_att, d)),
            full((1, d)),
        ],
        out_specs=pl.BlockSpec((1, MOBA_BLOCK, d), lambda b, i: (b, i, 0)),
        out_shape=jax.ShapeDtypeStruct(x.shape, F32),
        scratch_shapes=[
            pltpu.VMEM((MOBA_BLOCK, d), F32),
            pltpu.VMEM((MOBA_BLOCK, d), BF16),
            pltpu.VMEM((n_heads, MOBA_BLOCK, HEAD_DIM), F32),
            pltpu.VMEM((MOBA_BLOCK, d_att), F32),
            pltpu.VMEM((n_heads, HEAD_DIM, MOBA_BLOCK), F32),
            pltpu.VMEM((MAX_HEADS_PER_STAGE, n_blk, MOBA_BLOCK), F32),
            pltpu.VMEM((1, n_heads * seq, MOBA_BLOCK), F32),
            pltpu.VMEM((2, MAX_HEADS_PER_STAGE, SUBLANES, MOBA_BLOCK), F32),
        ],
        compiler_params=pltpu.CompilerParams(
            dimension_semantics=("arbitrary", "arbitrary"),
            vmem_limit_bytes=VMEM_LIMIT_BYTES),
        name="moba_stack",
    )(x, mod, norm_g, w_in, k, v_t, k_mean, w_out, final_g)


def kernel(x, c, mod_w, mod_b, norm_g, rg_w_in, rg_conv_w, rg_conv_b, rg_w_a, rg_b_a, rg_w_x,
           rg_b_x, rg_lambda, rg_w_out, kv_norm_g, kv_mod_w, kv_mod_b, w_kv, att_w_in,
           att_w_out, final_norm_g):
    bsz, seq, d = x.shape
    depth = mod_w.shape[0]
    n_a = rg_w_in.shape[0]
    n_b = att_w_in.shape[0]
    assert depth == n_a + n_b and seq % MOBA_BLOCK == 0
    d_att = w_kv.shape[1] // 2
    n_heads = d_att // HEAD_DIM
    n_blk = seq // MOBA_BLOCK

    mod = _modulation(c, mod_w, mod_b, tn=768)
    kv_mod = _modulation(c, kv_mod_w[None], kv_mod_b[None], tn=1024)

    row = lambda p: p.reshape(1, -1)
    for i in range(n_a):
        w_ax = jnp.concatenate([rg_w_a[i], rg_w_x[i]], axis=-1).astype(BF16)
        x = _rglru_layer(
            x, mod[i], row(norm_g[i]), rg_w_in[i].astype(BF16), rg_conv_w[i],
            row(rg_conv_b[i]), w_ax, row(rg_b_a[i]), row(rg_b_x[i]), row(rg_lambda[i]),
            rg_w_out[i].astype(BF16), t_len=128)

    k, v_t, k_mean = _shared_kv(x, kv_mod[0][:, None, :], row(kv_norm_g),
                                w_kv[:, :d_att].astype(BF16), w_kv[:, d_att:].T.astype(BF16),
                                tile=2 * MOBA_BLOCK)
    k_mean = k_mean.reshape(bsz, n_blk, n_heads, HEAD_DIM).transpose(0, 2, 1, 3)

    return _moba_stack(
        x, mod[n_a:].transpose(1, 0, 2), norm_g[n_a:, None, :], att_w_in.astype(BF16), k, v_t,
        k_mean, att_w_out.astype(BF16), row(final_norm_g))
```

```python
import functools

import jax
import jax.numpy as jnp
from jax import lax
from jax.experimental import pallas as pl
from jax.experimental.pallas import tpu as pltpu

EPS = 1e-6
RG_C = 8.0
HEAD_DIM = 128
MOBA_BLOCK = 256
MOBA_TOPK = 3
NEG_INF = -1e30
LOG2E = 1.4426950408889634
SUBLANES = 8
MXU_DIM = 256
VMEM_LIMIT_BYTES = 56 * 1024 * 1024

F32 = jnp.float32
BF16 = jnp.bfloat16
NT_DIMS = (((1,), (1,)), ((), ()))


def _sigmoid(z):
    return 1.0 / (1.0 + jnp.exp2(z * (-LOG2E)))


def _sqrt_nonneg(v):
    return jnp.where(v > 0.0, v * lax.rsqrt(v), 0.0)


def _norm_modulate(x, norm_g, shift, scale):
    ms = jnp.mean(x * x, axis=-1, keepdims=True)
    return x * lax.rsqrt(ms + EPS) * (norm_g * (1.0 + scale)) + shift


def _mod_kernel(c_ref, w_ref, b_ref, o_ref):
    c = c_ref[...]
    cs = (c * _sigmoid(c)).astype(BF16)
    w = w_ref[0].astype(BF16)
    o_ref[0] = jnp.dot(cs, w, preferred_element_type=F32) + b_ref[0]


def _modulation(c, w, b, tn):
    n_layers, d, n = w.shape
    bsz = c.shape[0]
    return pl.pallas_call(
        _mod_kernel,
        grid=(n_layers, n // tn),
        in_specs=[
            pl.BlockSpec((bsz, d), lambda l, j: (0, 0)),
            pl.BlockSpec((1, d, tn), lambda l, j: (l, 0, j)),
            pl.BlockSpec((1, 1, tn), lambda l, j: (l, 0, j)),
        ],
        out_specs=pl.BlockSpec((1, bsz, tn), lambda l, j: (l, 0, j)),
        out_shape=jax.ShapeDtypeStruct((n_layers, bsz, n), F32),
        compiler_params=pltpu.CompilerParams(
            dimension_semantics=("arbitrary", "arbitrary"),
            vmem_limit_bytes=VMEM_LIMIT_BYTES),
        name="adaln_mod",
    )(c, w, b.reshape(n_layers, 1, n))


def _tile_copies(hbm_ref, buf, sem, tile, slot, to_hbm):
    t_len, bsz, _ = buf.shape[1:]
    copies = []
    for b in range(bsz):
        hbm = hbm_ref.at[b, pl.ds(tile * t_len, t_len), :]
        vmem = buf.at[slot, :, b, :]
        src, dst = (vmem, hbm) if to_hbm else (hbm, vmem)
        copies.append(pltpu.make_async_copy(src, dst, sem.at[slot]))
    return copies


def _rglru_kernel(x_hbm, mod_ref, ng_ref, win_ref, cw_ref, cb_ref, wax_ref, ba_ref,
                  bx_ref, lam_ref, wout_ref, o_hbm,
                  xbuf, obuf, in_sem, out_sem, hs_s, y_s, utail, hstate):
    _, t_len, bsz, d = xbuf.shape
    n_heads, rb, _ = wax_ref.shape
    conv_w = cw_ref.shape[0]
    i = pl.program_id(0)
    n_tiles = x_hbm.shape[1] // t_len
    slot = i & 1

    @pl.when(i == 0)
    def _():
        for c in _tile_copies(x_hbm, xbuf, in_sem, 0, 0, to_hbm=False):
            c.start()
        utail[...] = jnp.zeros_like(utail)
        hstate[...] = jnp.zeros_like(hstate)

    @pl.when(i + 1 < n_tiles)
    def _():
        for c in _tile_copies(x_hbm, xbuf, in_sem, i + 1, 1 - slot, to_hbm=False):
            c.start()

    @pl.when(i >= 2)
    def _():
        for c in _tile_copies(o_hbm, obuf, out_sem, i - 2, slot, to_hbm=True):
            c.wait()

    for c in _tile_copies(x_hbm, xbuf, in_sem, i, slot, to_hbm=False):
        c.wait()

    mod = mod_ref[...]
    shift, scale, gate = mod[:, :d], mod[:, d:2 * d], mod[:, 2 * d:]
    norm_w = ng_ref[...] * (1.0 + scale)
    lam = lam_ref[...]
    softplus_neg_lam = jnp.maximum(-lam, 0.0) + jnp.log1p(jnp.exp(-jnp.abs(lam)))
    log2_a_per_r = (-RG_C * LOG2E) * softplus_neg_lam

    rows = t_len * bsz
    x3 = xbuf[slot]
    ms = jnp.mean(x3 * x3, axis=-1, keepdims=True)
    h3 = x3 * lax.rsqrt(ms + EPS) * norm_w + shift
    hb = h3.reshape(rows, d).astype(BF16)
    u3 = jnp.dot(hb, win_ref[:, :d], preferred_element_type=F32).reshape(t_len, bsz, d)

    upad = jnp.concatenate([utail[...], u3], axis=0)
    utail[...] = u3[t_len - (conv_w - 1):]
    uc3 = cb_ref[...] + cw_ref[conv_w - 1:conv_w, :] * u3
    for k in range(conv_w - 1):
        uc3 = uc3 + cw_ref[k:k + 1, :] * upad[k:k + t_len]
    uc = uc3.reshape(rows, d)

    for hh in range(n_heads):
        sl = slice(hh * rb, (hh + 1) * rb)
        uch = uc[:, sl]
        z = jnp.dot(uch.astype(BF16), wax_ref[hh], preferred_element_type=F32)
        r = _sigmoid(z[:, :rb] + ba_ref[:, sl])
        gi = _sigmoid(z[:, rb:] + bx_ref[:, sl])
        a = jnp.exp2(r * log2_a_per_r[:, sl])
        b_in = _sqrt_nonneg(1.0 - a * a) * (gi * uch)
        a3 = a.reshape(t_len, bsz, rb)
        b3 = b_in.reshape(t_len, bsz, rb)
        h_run = hstate[:, sl]
        for t in range(t_len):
            h_run = a3[t] * h_run + b3[t]
            hs_s[t, :, sl] = h_run
        hstate[:, sl] = h_run
        gpath = jnp.dot(hb, win_ref[:, d + hh * rb:d + (hh + 1) * rb], preferred_element_type=F32)
        y = hs_s[:, :, sl].reshape(rows, rb) * (gpath * _sigmoid(gpath))
        y_s[:, sl] = y.astype(BF16)

    proj = jnp.dot(y_s[...], wout_ref[...], preferred_element_type=F32)
    obuf[slot] = x3 + gate * proj.reshape(t_len, bsz, d)

    for c in _tile_copies(o_hbm, obuf, out_sem, i, slot, to_hbm=True):
        c.start()

    @pl.when(i == n_tiles - 1)
    def _():
        if n_tiles >= 2:
            for c in _tile_copies(o_hbm, obuf, out_sem, i - 1, 1 - slot, to_hbm=True):
                c.wait()
        for c in _tile_copies(o_hbm, obuf, out_sem, i, slot, to_hbm=True):
            c.wait()


def _rglru_layer(x, mod, norm_g, w_in, conv_w, conv_b, w_ax, b_a, b_x, lam, w_out, t_len):
    bsz, seq, d = x.shape
    assert bsz == SUBLANES and seq % t_len == 0 and t_len >= conv_w.shape[0] - 1
    n_heads, rb, _ = w_ax.shape
    full = lambda shape: pl.BlockSpec(shape, lambda i: (0,) * len(shape))
    return pl.pallas_call(
        _rglru_kernel,
        grid=(seq // t_len,),
        in_specs=[
            pl.BlockSpec(memory_space=pl.ANY),
            full((bsz, 3 * d)),
            full((1, d)),
            full((d, 2 * d)),
            full(conv_w.shape),
            full((1, d)),
            full((n_heads, rb, 2 * rb)),
            full((1, d)),
            full((1, d)),
            full((1, d)),
            full((d, d)),
        ],
        out_specs=pl.BlockSpec(memory_space=pl.ANY),
        out_shape=jax.ShapeDtypeStruct(x.shape, F32),
        scratch_shapes=[
            pltpu.VMEM((2, t_len, bsz, d), F32),
            pltpu.VMEM((2, t_len, bsz, d), F32),
            pltpu.SemaphoreType.DMA((2,)),
            pltpu.SemaphoreType.DMA((2,)),
            pltpu.VMEM((t_len, bsz, d), F32),
            pltpu.VMEM((t_len * bsz, d), BF16),
            pltpu.VMEM((conv_w.shape[0] - 1, bsz, d), F32),
            pltpu.VMEM((bsz, d), F32),
        ],
        compiler_params=pltpu.CompilerParams(
            dimension_semantics=("arbitrary",),
            vmem_limit_bytes=VMEM_LIMIT_BYTES),
        name="rglru_layer",
    )(x, mod, norm_g, w_in, conv_w, conv_b, w_ax, b_a, b_x, lam, w_out)


def _kv_kernel(x_ref, mod_ref, ng_ref, wk_ref, wvt_ref, k_ref, vt_ref, km_ref):
    d = x_ref.shape[2]
    n_heads = k_ref.shape[1]
    mod = mod_ref[0]
    h = _norm_modulate(x_ref[0], ng_ref[...], mod[:, :d], mod[:, d:]).astype(BF16)
    k = jnp.dot(h, wk_ref[...], preferred_element_type=F32)
    vt = lax.dot_general(wvt_ref[...], h, NT_DIMS, preferred_element_type=F32)
    for j in range(km_ref.shape[1]):
        km_ref[0, j] = jnp.mean(k[j * MOBA_BLOCK:(j + 1) * MOBA_BLOCK], axis=0, keepdims=True)
    for hd in range(n_heads):
        sl = slice(hd * HEAD_DIM, (hd + 1) * HEAD_DIM)
        k_ref[0, hd] = k[:, sl].astype(BF16)
        vt_ref[0, hd] = vt[sl, :].astype(BF16)


def _shared_kv(x, mod, norm_g, w_k, w_v_t, tile):
    bsz, seq, d = x.shape
    d_att = w_k.shape[1]
    n_heads = d_att // HEAD_DIM
    n_blk = seq // MOBA_BLOCK
    blk_per_tile = tile // MOBA_BLOCK
    assert tile % MOBA_BLOCK == 0 and seq % tile == 0
    return pl.pallas_call(
        _kv_kernel,
        grid=(bsz, seq // tile),
        in_specs=[
            pl.BlockSpec((1, tile, d), lambda b, i: (b, i, 0)),
            pl.BlockSpec((1, 1, 2 * d), lambda b, i: (b, 0, 0)),
            pl.BlockSpec((1, d), lambda b, i: (0, 0)),
            pl.BlockSpec((d, d_att), lambda b, i: (0, 0)),
            pl.BlockSpec((d_att, d), lambda b, i: (0, 0)),
        ],
        out_specs=[
            pl.BlockSpec((1, n_heads, tile, HEAD_DIM), lambda b, i: (b, 0, i, 0)),
            pl.BlockSpec((1, n_heads, HEAD_DIM, tile), lambda b, i: (b, 0, 0, i)),
            pl.BlockSpec((1, blk_per_tile, 1, d_att), lambda b, i: (b, i, 0, 0)),
        ],
        out_shape=[jax.ShapeDtypeStruct((bsz, n_heads, seq, HEAD_DIM), BF16),
                   jax.ShapeDtypeStruct((bsz, n_heads, HEAD_DIM, seq), BF16),
                   jax.ShapeDtypeStruct((bsz, n_blk, 1, d_att), F32)],
        compiler_params=pltpu.CompilerParams(
            dimension_semantics=("arbitrary", "arbitrary"),
            vmem_limit_bytes=VMEM_LIMIT_BYTES),
        name="shared_kv",
    )(x, mod, norm_g, w_k, w_v_t)


def _select_blocks(gate_t, n_past, sel_s):
    blk = lax.broadcasted_iota(jnp.int32, gate_t.shape, 0)
    g = jnp.where(blk < n_past, gate_t, NEG_INF)
    sel_s[...] = g
    rank = jnp.zeros(gate_t.shape, jnp.int32)
    for j in range(n_past):
        gj = jnp.broadcast_to(sel_s[j:j + 1, :], gate_t.shape)
        ahead = (gj > g) | ((gj == g) & (blk > j))
        rank = rank + ahead.astype(jnp.int32)
    sel_s[...] = jnp.where((rank < MOBA_TOPK) & (blk < n_past), 0.0, NEG_INF)


def _fold_rows(a, op):
    return op(a.reshape(a.shape[0] // SUBLANES, SUBLANES, a.shape[1]), axis=0)


MAX_HEADS_PER_STAGE = 8


def _heads_per_stage(n_past, slot_rows):
    per_stage = 1
    while (per_stage < MAX_HEADS_PER_STAGE
           and 2 * per_stage * (n_past + 1) * MOBA_BLOCK <= slot_rows):
        per_stage *= 2
    return per_stage


def _head_scores(hd, slot, lane, n_past, q_s, k_ref, km_ref, sel_s, sc_s, m_s):
    qf = q_s[hd]
    qh = (qf * (HEAD_DIM ** -0.5 * LOG2E)).astype(BF16)
    masked = n_past > MOBA_TOPK
    if masked:
        gate_t = lax.dot_general(km_ref[0, hd].astype(BF16), qf.astype(BF16), NT_DIMS,
                                 preferred_element_type=F32)
        sel_s = sel_s.at[lane]
        _select_blocks(gate_t, n_past, sel_s)
    n_keys = (n_past + 1) * MOBA_BLOCK
    s_all = lax.dot_general(k_ref[0, hd, 0:n_keys, :], qh, NT_DIMS,
                            preferred_element_type=F32)
    m8 = None
    for j in range(n_past + 1):
        rows = slice(j * MOBA_BLOCK, (j + 1) * MOBA_BLOCK)
        s = s_all[rows]
        if j == n_past:
            key = lax.broadcasted_iota(jnp.int32, s.shape, 0)
            qry = lax.broadcasted_iota(jnp.int32, s.shape, 1)
            s = jnp.where(key <= qry, s, NEG_INF)
        elif masked:
            s = s + sel_s[j:j + 1, :]
        sc_s[slot, lane * n_keys + j * MOBA_BLOCK:lane * n_keys + (j + 1) * MOBA_BLOCK] = s
        smax = _fold_rows(s, jnp.max)
        m8 = smax if m8 is None else jnp.maximum(m8, smax)
    m_s[slot, lane] = m8


def _head_output(hd, slot, lane, n_past, vt_ref, sc_s, m_s, o_s):
    n_keys = (n_past + 1) * MOBA_BLOCK
    m = jnp.max(m_s[slot, lane], axis=0, keepdims=True)
    p = jnp.exp2(sc_s[slot, lane * n_keys:(lane + 1) * n_keys] - m)
    denom = jnp.sum(_fold_rows(p, jnp.sum), axis=0, keepdims=True)
    o_t = jnp.dot(vt_ref[0, hd, :, 0:n_keys], p.astype(BF16),
                  preferred_element_type=F32)
    o_s[hd] = o_t * (1.0 / denom)


def _moba_kernel(x_ref, mod_ref, ng_ref, win_ref, k_ref, vt_ref, km_ref, wout_ref, fg_ref,
                 o_ref, x_s, hb_s, q_s, g_s, o_s, sel_s, sc_s, m_s):
    d = x_ref.shape[2]
    n_layers = win_ref.shape[0]
    n_heads = k_ref.shape[1]
    n_blk = km_ref.shape[2]
    d_att = n_heads * HEAD_DIM
    qb = pl.program_id(1)
    scores = functools.partial(_head_scores, q_s=q_s, k_ref=k_ref, km_ref=km_ref, sel_s=sel_s,
                               sc_s=sc_s, m_s=m_s)
    output = functools.partial(_head_output, vt_ref=vt_ref, sc_s=sc_s, m_s=m_s, o_s=o_s)

    def project_q(lyr):
        mod = mod_ref[0, pl.ds(lyr, 1), :]
        hb = _norm_modulate(x_s[...], ng_ref[lyr], mod[:, :d], mod[:, d:2 * d]).astype(BF16)
        hb_s[...] = hb
        q = jnp.dot(hb, win_ref[lyr, :, :d_att], preferred_element_type=F32)
        for hd in range(n_heads):
            q_s[hd] = q[:, hd * HEAD_DIM:(hd + 1) * HEAD_DIM]

    def attend_and_project(n_past, lyr):
        per_stage = _heads_per_stage(n_past, sc_s.shape[1])
        n_groups = n_heads // per_stage
        assert sc_s.shape[0] >= min(n_groups, 2)

        def stage(fn, group):
            for lane in range(per_stage):
                fn(group * per_stage + lane, group & 1, lane, n_past)

        stage(scores, 0)
        g_s[...] = jnp.dot(hb_s[...], win_ref[lyr, :, d_att:], preferred_element_type=F32)

        def step(group, carry):
            stage(output, group - 1)
            stage(scores, group)
            return carry

        lax.fori_loop(1, n_groups, step, 0)
        stage(output, n_groups - 1)

        ys = []
        for hd in range(n_heads):
            gp = g_s[:, hd * HEAD_DIM:(hd + 1) * HEAD_DIM]
            ys.append((o_s[hd].T * (gp * _sigmoid(gp))).astype(BF16))
        early = (n_heads - 1) * HEAD_DIM // MXU_DIM * MXU_DIM
        proj = jnp.dot(jnp.concatenate(ys[:early // HEAD_DIM], axis=1), wout_ref[lyr, :early, :],
                       preferred_element_type=F32)
        proj = proj + jnp.dot(jnp.concatenate(ys[early // HEAD_DIM:], axis=1),
                              wout_ref[lyr, early:, :], preferred_element_type=F32)
        gate = mod_ref[0, pl.ds(lyr, 1), 2 * d:]
        x_s[...] = x_s[...] + gate * proj

    x_s[...] = x_ref[0]

    def one_layer(lyr, carry):
        project_q(lyr)
        for n_past in range(n_blk):
            pl.when(qb == n_past)(functools.partial(attend_and_project, n_past, lyr))
        return carry

    lax.fori_loop(0, n_layers, one_layer, 0)
    out = x_s[...]
    ms = jnp.mean(out * out, axis=-1, keepdims=True)
    o_ref[0] = out * lax.rsqrt(ms + EPS) * fg_ref[...]


def _moba_stack(x, mod, norm_g, w_in, k, v_t, k_mean, w_out, final_g):
    bsz, seq, d = x.shape
    n_layers = w_in.shape[0]
    n_heads = k.shape[1]
    d_att = n_heads * HEAD_DIM
    n_blk = seq // MOBA_BLOCK
    full = lambda shape: pl.BlockSpec(shape, lambda b, i: (0,) * len(shape))
    per_batch = lambda shape, **kw: pl.BlockSpec(
        shape, lambda b, i: (b,) + (0,) * (len(shape) - 1), **kw)
    return pl.pallas_call(
        _moba_kernel,
        grid=(bsz, n_blk),
        in_specs=[
            pl.BlockSpec((1, MOBA_BLOCK, d), lambda b, i: (b, i, 0)),
            per_batch((1, n_layers, 3 * d)),
            full((n_layers, 1, d)),
            full((n_layers, d, 2 * d_att)),
            per_batch((1, n_heads, seq, HEAD_DIM), pipeline_mode=pl.Buffered(1)),
            per_batch((1, n_heads, HEAD_DIM, seq), pipeline_mode=pl.Buffered(1)),
            per_batch((1, n_heads, n_blk, HEAD_DIM)),
            full((n_layers, d_att, d)),
            full((1, d)),
        ],
        out_specs=pl.BlockSpec((1, MOBA_BLOCK, d), lambda b, i: (b, i, 0)),
        out_shape=jax.ShapeDtypeStruct(x.shape, F32),
        scratch_shapes=[
            pltpu.VMEM((MOBA_BLOCK, d), F32),
            pltpu.VMEM((MOBA_BLOCK, d), BF16),
            pltpu.VMEM((n_heads, MOBA_BLOCK, HEAD_DIM), F32),
            pltpu.VMEM((MOBA_BLOCK, d_att), F32),
            pltpu.VMEM((n_heads, HEAD_DIM, MOBA_BLOCK), F32),
            pltpu.VMEM((MAX_HEADS_PER_STAGE, n_blk, MOBA_BLOCK), F32),
            pltpu.VMEM((1, n_heads * seq, MOBA_BLOCK), F32),
            pltpu.VMEM((2, MAX_HEADS_PER_STAGE, SUBLANES, MOBA_BLOCK), F32),
        ],
        compiler_params=pltpu.CompilerParams(
            dimension_semantics=("arbitrary", "arbitrary"),
            vmem_limit_bytes=VMEM_LIMIT_BYTES),
        name="moba_stack",
    )(x, mod, norm_g, w_in, k, v_t, k_mean, w_out, final_g)


def kernel(x, c, mod_w, mod_b, norm_g, rg_w_in, rg_conv_w, rg_conv_b, rg_w_a, rg_b_a, rg_w_x,
           rg_b_x, rg_lambda, rg_w_out, kv_norm_g, kv_mod_w, kv_mod_b, w_kv, att_w_in,
           att_w_out, final_norm_g):
    bsz, seq, d = x.shape
    depth = mod_w.shape[0]
    n_a = rg_w_in.shape[0]
    n_b = att_w_in.shape[0]
    assert depth == n_a + n_b and seq % MOBA_BLOCK == 0
    d_att = w_kv.shape[1] // 2
    n_heads = d_att // HEAD_DIM
    n_blk = seq // MOBA_BLOCK

    mod = _modulation(c, mod_w, mod_b, tn=d * 3 // 2)
    kv_mod = _modulation(c, kv_mod_w[None], kv_mod_b[None], tn=d)

    row = lambda p: p.reshape(1, -1)
    for i in range(n_a):
        w_ax = jnp.concatenate([rg_w_a[i], rg_w_x[i]], axis=-1).astype(BF16)
        x = _rglru_layer(
            x, mod[i], row(norm_g[i]), rg_w_in[i].astype(BF16), rg_conv_w[i],
            row(rg_conv_b[i]), w_ax, row(rg_b_a[i]), row(rg_b_x[i]), row(rg_lambda[i]),
            rg_w_out[i].astype(BF16), t_len=128)

    k, v_t, k_mean = _shared_kv(x, kv_mod[0][:, None, :], row(kv_norm_g),
                                w_kv[:, :d_att].astype(BF16), w_kv[:, d_att:].T.astype(BF16),
                                tile=2 * MOBA_BLOCK)
    k_mean = k_mean.reshape(bsz, n_blk, n_heads, HEAD_DIM).transpose(0, 2, 1, 3)

    return _moba_stack(
        x, mod[n_a:].transpose(1, 0, 2), norm_g[n_a:, None, :], att_w_in.astype(BF16), k, v_t,
        k_mean, att_w_out.astype(BF16), row(final_norm_g))
```

```python
import functools

import jax
import jax.numpy as jnp
from jax import lax
from jax.experimental import pallas as pl
from jax.experimental.pallas import tpu as pltpu

EPS = 1e-6
RG_C = 8.0
HEAD_DIM = 128
MOBA_BLOCK = 256
MOBA_TOPK = 3
NEG_INF = -1e30
LOG2E = 1.4426950408889634
SUBLANES = 8
MXU_DIM = 256
VMEM_LIMIT_BYTES = 56 * 1024 * 1024

F32 = jnp.float32
BF16 = jnp.bfloat16
NT_DIMS = (((1,), (1,)), ((), ()))


def _sigmoid(z):
    return 1.0 / (1.0 + jnp.exp2(z * (-LOG2E)))


def _sqrt_nonneg(v):
    return jnp.where(v > 0.0, v * lax.rsqrt(v), 0.0)


def _norm_modulate(x, norm_g, shift, scale):
    ms = jnp.mean(x * x, axis=-1, keepdims=True)
    return x * lax.rsqrt(ms + EPS) * (norm_g * (1.0 + scale)) + shift


def _mod_kernel(c_ref, w_ref, b_ref, o_ref):
    c = c_ref[...]
    cs = (c * _sigmoid(c)).astype(BF16)
    w = w_ref[0].astype(BF16)
    o_ref[0] = jnp.dot(cs, w, preferred_element_type=F32) + b_ref[0]


def _modulation(c, w, b, tn):
    n_layers, d, n = w.shape
    bsz = c.shape[0]
    return pl.pallas_call(
        _mod_kernel,
        grid=(n_layers, n // tn),
        in_specs=[
            pl.BlockSpec((bsz, d), lambda l, j: (0, 0)),
            pl.BlockSpec((1, d, tn), lambda l, j: (l, 0, j)),
            pl.BlockSpec((1, 1, tn), lambda l, j: (l, 0, j)),
        ],
        out_specs=pl.BlockSpec((1, bsz, tn), lambda l, j: (l, 0, j)),
        out_shape=jax.ShapeDtypeStruct((n_layers, bsz, n), F32),
        compiler_params=pltpu.CompilerParams(
            dimension_semantics=("arbitrary", "arbitrary"),
            vmem_limit_bytes=VMEM_LIMIT_BYTES),
        name="adaln_mod",
    )(c, w, b.reshape(n_layers, 1, n))


def _tile_copies(hbm_ref, buf, sem, tile, slot, to_hbm):
    t_len, bsz, _ = buf.shape[1:]
    copies = []
    for b in range(bsz):
        hbm = hbm_ref.at[b, pl.ds(tile * t_len, t_len), :]
        vmem = buf.at[slot, :, b, :]
        src, dst = (vmem, hbm) if to_hbm else (hbm, vmem)
        copies.append(pltpu.make_async_copy(src, dst, sem.at[slot]))
    return copies


def _rglru_kernel(x_hbm, mod_ref, ng_ref, win_ref, cw_ref, cb_ref, wax_ref, ba_ref,
                  bx_ref, lam_ref, wout_ref, o_hbm,
                  xbuf, obuf, in_sem, out_sem, hs_s, y_s, utail, hstate):
    _, t_len, bsz, d = xbuf.shape
    n_layers, n_heads, rb, _ = wax_ref.shape
    conv_w = cw_ref.shape[1]
    i = pl.program_id(0)
    n_tiles = x_hbm.shape[1] // t_len
    slot = i & 1

    @pl.when(i == 0)
    def _():
        for c in _tile_copies(x_hbm, xbuf, in_sem, 0, 0, to_hbm=False):
            c.start()
        utail[...] = jnp.zeros_like(utail)
        hstate[...] = jnp.zeros_like(hstate)

    @pl.when(i + 1 < n_tiles)
    def _():
        for c in _tile_copies(x_hbm, xbuf, in_sem, i + 1, 1 - slot, to_hbm=False):
            c.start()

    @pl.when(i >= 2)
    def _():
        for c in _tile_copies(o_hbm, obuf, out_sem, i - 2, slot, to_hbm=True):
            c.wait()

    for c in _tile_copies(x_hbm, xbuf, in_sem, i, slot, to_hbm=False):
        c.wait()

    rows = t_len * bsz

    def layer(lyr, x3):
        mod = mod_ref[lyr]
        shift, scale, gate = mod[:, :d], mod[:, d:2 * d], mod[:, 2 * d:]
        lam = lam_ref[lyr]
        softplus_neg_lam = jnp.maximum(-lam, 0.0) + jnp.log1p(jnp.exp(-jnp.abs(lam)))
        log2_a_per_r = (-RG_C * LOG2E) * softplus_neg_lam

        ms = jnp.mean(x3 * x3, axis=-1, keepdims=True)
        h3 = x3 * lax.rsqrt(ms + EPS) * (ng_ref[lyr] * (1.0 + scale)) + shift
        hb = h3.reshape(rows, d).astype(BF16)
        u3 = jnp.dot(hb, win_ref[lyr, :, :d], preferred_element_type=F32).reshape(t_len, bsz, d)

        upad = jnp.concatenate([utail[lyr], u3], axis=0)
        utail[lyr] = u3[t_len - (conv_w - 1):]
        uc3 = cb_ref[lyr] + cw_ref[lyr, conv_w - 1:conv_w, :] * u3
        for k in range(conv_w - 1):
            uc3 = uc3 + cw_ref[lyr, k:k + 1, :] * upad[k:k + t_len]
        uc = uc3.reshape(rows, d)

        for hh in range(n_heads):
            sl = slice(hh * rb, (hh + 1) * rb)
            uch = uc[:, sl]
            z = jnp.dot(uch.astype(BF16), wax_ref[lyr, hh], preferred_element_type=F32)
            r = _sigmoid(z[:, :rb] + ba_ref[lyr, :, sl])
            gi = _sigmoid(z[:, rb:] + bx_ref[lyr, :, sl])
            a = jnp.exp2(r * log2_a_per_r[:, sl])
            b_in = _sqrt_nonneg(1.0 - a * a) * (gi * uch)
            a3 = a.reshape(t_len, bsz, rb)
            b3 = b_in.reshape(t_len, bsz, rb)
            h_run = hstate[lyr, :, sl]
            for t in range(t_len):
                h_run = a3[t] * h_run + b3[t]
                hs_s[t, :, sl] = h_run
            hstate[lyr, :, sl] = h_run
            gpath = jnp.dot(hb, win_ref[lyr, :, d + hh * rb:d + (hh + 1) * rb],
                            preferred_element_type=F32)
            y = hs_s[:, :, sl].reshape(rows, rb) * (gpath * _sigmoid(gpath))
            y_s[:, sl] = y.astype(BF16)

        proj = jnp.dot(y_s[...], wout_ref[lyr], preferred_element_type=F32)
        return x3 + gate * proj.reshape(t_len, bsz, d)

    x3 = xbuf[slot]
    for lyr in range(n_layers):
        x3 = layer(lyr, x3)
    obuf[slot] = x3

    for c in _tile_copies(o_hbm, obuf, out_sem, i, slot, to_hbm=True):
        c.start()

    @pl.when(i == n_tiles - 1)
    def _():
        if n_tiles >= 2:
            for c in _tile_copies(o_hbm, obuf, out_sem, i - 1, 1 - slot, to_hbm=True):
                c.wait()
        for c in _tile_copies(o_hbm, obuf, out_sem, i, slot, to_hbm=True):
            c.wait()


def _rglru_stack(x, mod, norm_g, w_in, conv_w, conv_b, w_ax, b_a, b_x, lam, w_out, t_len):
    bsz, seq, d = x.shape
    n_conv = conv_w.shape[1]
    assert bsz == SUBLANES and seq % t_len == 0 and t_len >= n_conv - 1
    n_layers = w_in.shape[0]
    full = lambda a: pl.BlockSpec(a.shape, lambda i: (0,) * a.ndim)
    return pl.pallas_call(
        _rglru_kernel,
        grid=(seq // t_len,),
        in_specs=[pl.BlockSpec(memory_space=pl.ANY)] + [
            full(a) for a in (mod, norm_g, w_in, conv_w, conv_b, w_ax, b_a, b_x, lam, w_out)],
        out_specs=pl.BlockSpec(memory_space=pl.ANY),
        out_shape=jax.ShapeDtypeStruct(x.shape, F32),
        scratch_shapes=[
            pltpu.VMEM((2, t_len, bsz, d), F32),
            pltpu.VMEM((2, t_len, bsz, d), F32),
            pltpu.SemaphoreType.DMA((2,)),
            pltpu.SemaphoreType.DMA((2,)),
            pltpu.VMEM((t_len, bsz, d), F32),
            pltpu.VMEM((t_len * bsz, d), BF16),
            pltpu.VMEM((n_layers, n_conv - 1, bsz, d), F32),
            pltpu.VMEM((n_layers, bsz, d), F32),
        ],
        compiler_params=pltpu.CompilerParams(
            dimension_semantics=("arbitrary",),
            vmem_limit_bytes=VMEM_LIMIT_BYTES),
        name="rglru_stack",
    )(x, mod, norm_g, w_in, conv_w, conv_b, w_ax, b_a, b_x, lam, w_out)


def _kv_kernel(x_ref, mod_ref, ng_ref, wk_ref, wvt_ref, k_ref, vt_ref, km_ref):
    d = x_ref.shape[2]
    n_heads = k_ref.shape[1]
    mod = mod_ref[0]
    h = _norm_modulate(x_ref[0], ng_ref[...], mod[:, :d], mod[:, d:]).astype(BF16)
    k = jnp.dot(h, wk_ref[...], preferred_element_type=F32)
    vt = lax.dot_general(wvt_ref[...], h, NT_DIMS, preferred_element_type=F32)
    for j in range(km_ref.shape[1]):
        km_ref[0, j] = jnp.mean(k[j * MOBA_BLOCK:(j + 1) * MOBA_BLOCK], axis=0, keepdims=True)
    for hd in range(n_heads):
        sl = slice(hd * HEAD_DIM, (hd + 1) * HEAD_DIM)
        k_ref[0, hd] = k[:, sl].astype(BF16)
        vt_ref[0, hd] = vt[sl, :].astype(BF16)


def _shared_kv(x, mod, norm_g, w_k, w_v_t, tile):
    bsz, seq, d = x.shape
    d_att = w_k.shape[1]
    n_heads = d_att // HEAD_DIM
    n_blk = seq // MOBA_BLOCK
    blk_per_tile = tile // MOBA_BLOCK
    assert tile % MOBA_BLOCK == 0 and seq % tile == 0
    return pl.pallas_call(
        _kv_kernel,
        grid=(bsz, seq // tile),
        in_specs=[
            pl.BlockSpec((1, tile, d), lambda b, i: (b, i, 0)),
            pl.BlockSpec((1, 1, 2 * d), lambda b, i: (b, 0, 0)),
            pl.BlockSpec((1, d), lambda b, i: (0, 0)),
            pl.BlockSpec((d, d_att), lambda b, i: (0, 0)),
            pl.BlockSpec((d_att, d), lambda b, i: (0, 0)),
        ],
        out_specs=[
            pl.BlockSpec((1, n_heads, tile, HEAD_DIM), lambda b, i: (b, 0, i, 0)),
            pl.BlockSpec((1, n_heads, HEAD_DIM, tile), lambda b, i: (b, 0, 0, i)),
            pl.BlockSpec((1, blk_per_tile, 1, d_att), lambda b, i: (b, i, 0, 0)),
        ],
        out_shape=[jax.ShapeDtypeStruct((bsz, n_heads, seq, HEAD_DIM), BF16),
                   jax.ShapeDtypeStruct((bsz, n_heads, HEAD_DIM, seq), BF16),
                   jax.ShapeDtypeStruct((bsz, n_blk, 1, d_att), F32)],
        compiler_params=pltpu.CompilerParams(
            dimension_semantics=("arbitrary", "arbitrary"),
            vmem_limit_bytes=VMEM_LIMIT_BYTES),
        name="shared_kv",
    )(x, mod, norm_g, w_k, w_v_t)


def _select_blocks(gate_t, n_past, sel_s):
    blk = lax.broadcasted_iota(jnp.int32, gate_t.shape, 0)
    g = jnp.where(blk < n_past, gate_t, NEG_INF)
    sel_s[...] = g
    rank = jnp.zeros(gate_t.shape, jnp.int32)
    for j in range(n_past):
        gj = jnp.broadcast_to(sel_s[j:j + 1, :], gate_t.shape)
        ahead = (gj > g) | ((gj == g) & (blk > j))
        rank = rank + ahead.astype(jnp.int32)
    sel_s[...] = jnp.where((rank < MOBA_TOPK) & (blk < n_past), 0.0, NEG_INF)


def _fold_rows(a, op):
    return op(a.reshape(a.shape[0] // SUBLANES, SUBLANES, a.shape[1]), axis=0)


MAX_HEADS_PER_STAGE = 8


def _heads_per_stage(n_past, slot_rows):
    per_stage = 1
    while (per_stage < MAX_HEADS_PER_STAGE
           and 2 * per_stage * (n_past + 1) * MOBA_BLOCK <= slot_rows):
        per_stage *= 2
    return per_stage


def _head_scores(hd, slot, lane, n_past, q_s, k_ref, km_ref, sel_s, sc_s, m_s):
    qf = q_s[hd]
    qh = (qf * (HEAD_DIM ** -0.5 * LOG2E)).astype(BF16)
    masked = n_past > MOBA_TOPK
    if masked:
        gate_t = lax.dot_general(km_ref[0, hd].astype(BF16), qf.astype(BF16), NT_DIMS,
                                 preferred_element_type=F32)
        sel_s = sel_s.at[lane]
        _select_blocks(gate_t, n_past, sel_s)
    n_keys = (n_past + 1) * MOBA_BLOCK
    s_all = lax.dot_general(k_ref[0, hd, 0:n_keys, :], qh, NT_DIMS,
                            preferred_element_type=F32)
    m8 = None
    for j in range(n_past + 1):
        rows = slice(j * MOBA_BLOCK, (j + 1) * MOBA_BLOCK)
        s = s_all[rows]
        if j == n_past:
            key = lax.broadcasted_iota(jnp.int32, s.shape, 0)
            qry = lax.broadcasted_iota(jnp.int32, s.shape, 1)
            s = jnp.where(key <= qry, s, NEG_INF)
        elif masked:
            s = s + sel_s[j:j + 1, :]
        sc_s[slot, lane * n_keys + j * MOBA_BLOCK:lane * n_keys + (j + 1) * MOBA_BLOCK] = s
        smax = _fold_rows(s, jnp.max)
        m8 = smax if m8 is None else jnp.maximum(m8, smax)
    m_s[slot, lane] = m8


def _head_output(hd, slot, lane, n_past, vt_ref, sc_s, m_s, o_s):
    n_keys = (n_past + 1) * MOBA_BLOCK
    m = jnp.max(m_s[slot, lane], axis=0, keepdims=True)
    p = jnp.exp2(sc_s[slot, lane * n_keys:(lane + 1) * n_keys] - m)
    denom = jnp.sum(_fold_rows(p, jnp.sum), axis=0, keepdims=True)
    o_t = jnp.dot(vt_ref[0, hd, :, 0:n_keys], p.astype(BF16),
                  preferred_element_type=F32)
    o_s[hd] = o_t * (1.0 / denom)


def _moba_kernel(x_ref, mod_ref, ng_ref, win_ref, k_ref, vt_ref, km_ref, wout_ref, fg_ref,
                 o_ref, x_s, hb_s, q_s, g_s, o_s, sel_s, sc_s, m_s):
    d = x_ref.shape[2]
    n_layers = win_ref.shape[0]
    n_heads = k_ref.shape[1]
    n_blk = km_ref.shape[2]
    d_att = n_heads * HEAD_DIM
    qb = pl.program_id(1)
    scores = functools.partial(_head_scores, q_s=q_s, k_ref=k_ref, km_ref=km_ref, sel_s=sel_s,
                               sc_s=sc_s, m_s=m_s)
    output = functools.partial(_head_output, vt_ref=vt_ref, sc_s=sc_s, m_s=m_s, o_s=o_s)

    def project_q(lyr):
        mod = mod_ref[0, pl.ds(lyr, 1), :]
        hb = _norm_modulate(x_s[...], ng_ref[lyr], mod[:, :d], mod[:, d:2 * d]).astype(BF16)
        hb_s[...] = hb
        q = jnp.dot(hb, win_ref[lyr, :, :d_att], preferred_element_type=F32)
        for hd in range(n_heads):
            q_s[hd] = q[:, hd * HEAD_DIM:(hd + 1) * HEAD_DIM]

    def attend_and_project(n_past, lyr):
        per_stage = _heads_per_stage(n_past, sc_s.shape[1])
        n_groups = n_heads // per_stage
        assert sc_s.shape[0] >= min(n_groups, 2)

        def stage(fn, group):
            for lane in range(per_stage):
                fn(group * per_stage + lane, group & 1, lane, n_past)

        stage(scores, 0)
        g_s[...] = jnp.dot(hb_s[...], win_ref[lyr, :, d_att:], preferred_element_type=F32)

        def step(group, carry):
            stage(output, group - 1)
            stage(scores, group)
            return carry

        lax.fori_loop(1, n_groups, step, 0)
        stage(output, n_groups - 1)

        ys = []
        for hd in range(n_heads):
            gp = g_s[:, hd * HEAD_DIM:(hd + 1) * HEAD_DIM]
            ys.append((o_s[hd].T * (gp * _sigmoid(gp))).astype(BF16))
        early = (n_heads - 1) * HEAD_DIM // MXU_DIM * MXU_DIM
        proj = jnp.dot(jnp.concatenate(ys[:early // HEAD_DIM], axis=1), wout_ref[lyr, :early, :],
                       preferred_element_type=F32)
        proj = proj + jnp.dot(jnp.concatenate(ys[early // HEAD_DIM:], axis=1),
                              wout_ref[lyr, early:, :], preferred_element_type=F32)
        gate = mod_ref[0, pl.ds(lyr, 1), 2 * d:]
        x_s[...] = x_s[...] + gate * proj

    x_s[...] = x_ref[0]

    def one_layer(lyr, carry):
        project_q(lyr)
        for n_past in range(n_blk):
            pl.when(qb == n_past)(functools.partial(attend_and_project, n_past, lyr))
        return carry

    lax.fori_loop(0, n_layers, one_layer, 0)
    out = x_s[...]
    ms = jnp.mean(out * out, axis=-1, keepdims=True)
    o_ref[0] = out * lax.rsqrt(ms + EPS) * fg_ref[...]


def _moba_stack(x, mod, norm_g, w_in, k, v_t, k_mean, w_out, final_g):
    bsz, seq, d = x.shape
    n_layers = w_in.shape[0]
    n_heads = k.shape[1]
    d_att = n_heads * HEAD_DIM
    n_blk = seq // MOBA_BLOCK
    full = lambda shape: pl.BlockSpec(shape, lambda b, i: (0,) * len(shape))
    per_batch = lambda shape, **kw: pl.BlockSpec(
        shape, lambda b, i: (b,) + (0,) * (len(shape) - 1), **kw)
    return pl.pallas_call(
        _moba_kernel,
        grid=(bsz, n_blk),
        in_specs=[
            pl.BlockSpec((1, MOBA_BLOCK, d), lambda b, i: (b, i, 0)),
            per_batch((1, n_layers, 3 * d)),
            full((n_layers, 1, d)),
            full((n_layers, d, 2 * d_att)),
            per_batch((1, n_heads, seq, HEAD_DIM), pipeline_mode=pl.Buffered(1)),
            per_batch((1, n_heads, HEAD_DIM, seq), pipeline_mode=pl.Buffered(1)),
            per_batch((1, n_heads, n_blk, HEAD_DIM)),
            full((n_layers, d_att, d)),
            full((1, d)),
        ],
        out_specs=pl.BlockSpec((1, MOBA_BLOCK, d), lambda b, i: (b, i, 0)),
        out_shape=jax.ShapeDtypeStruct(x.shape, F32),
        scratch_shapes=[
            pltpu.VMEM((MOBA_BLOCK, d), F32),
            pltpu.VMEM((MOBA_BLOCK, d), BF16),
            pltpu.VMEM((n_heads, MOBA_BLOCK, HEAD_DIM), F32),
            pltpu.VMEM((MOBA_BLOCK, d_att), F32),
            pltpu.VMEM((n_heads, HEAD_DIM, MOBA_BLOCK), F32),
            pltpu.VMEM((MAX_HEADS_PER_STAGE, n_blk, MOBA_BLOCK), F32),
            pltpu.VMEM((1, n_heads * seq, MOBA_BLOCK), F32),
            pltpu.VMEM((2, MAX_HEADS_PER_STAGE, SUBLANES, MOBA_BLOCK), F32),
        ],
        compiler_params=pltpu.CompilerParams(
            dimension_semantics=("arbitrary", "arbitrary"),
            vmem_limit_bytes=VMEM_LIMIT_BYTES),
        name="moba_stack",
    )(x, mod, norm_g, w_in, k, v_t, k_mean, w_out, final_g)


def kernel(x, c, mod_w, mod_b, norm_g, rg_w_in, rg_conv_w, rg_conv_b, rg_w_a, rg_b_a, rg_w_x,
           rg_b_x, rg_lambda, rg_w_out, kv_norm_g, kv_mod_w, kv_mod_b, w_kv, att_w_in,
           att_w_out, final_norm_g):
    bsz, seq, d = x.shape
    depth = mod_w.shape[0]
    n_a = rg_w_in.shape[0]
    n_b = att_w_in.shape[0]
    assert depth == n_a + n_b and seq % MOBA_BLOCK == 0
    d_att = w_kv.shape[1] // 2
    n_heads = d_att // HEAD_DIM
    n_blk = seq // MOBA_BLOCK

    mod = _modulation(c, mod_w, mod_b, tn=d * 3 // 2)
    kv_mod = _modulation(c, kv_mod_w[None], kv_mod_b[None], tn=d)

    row = lambda p: p.reshape(1, -1)
    rows = lambda p: p[:, None, :]
    w_ax = jnp.concatenate([rg_w_a, rg_w_x], axis=-1).astype(BF16)
    x = _rglru_stack(
        x, mod[:n_a], rows(norm_g[:n_a]), rg_w_in.astype(BF16), rg_conv_w, rows(rg_conv_b), w_ax,
        rows(rg_b_a), rows(rg_b_x), rows(rg_lambda), rg_w_out.astype(BF16), t_len=128)

    k, v_t, k_mean = _shared_kv(x, kv_mod[0][:, None, :], row(kv_norm_g),
                                w_kv[:, :d_att].astype(BF16), w_kv[:, d_att:].T.astype(BF16),
                                tile=4 * MOBA_BLOCK)
    k_mean = k_mean.reshape(bsz, n_blk, n_heads, HEAD_DIM).transpose(0, 2, 1, 3)

    return _moba_stack(
        x, mod[n_a:].transpose(1, 0, 2), norm_g[n_a:, None, :], att_w_in.astype(BF16), k, v_t,
        k_mean, att_w_out.astype(BF16), row(final_norm_g))
```

```python
import functools

import jax
import jax.numpy as jnp
from jax import lax
from jax.experimental import pallas as pl
from jax.experimental.pallas import tpu as pltpu

EPS = 1e-6
RG_C = 8.0
HEAD_DIM = 128
MOBA_BLOCK = 256
MOBA_TOPK = 3
NEG_INF = -1e30
LOG2E = 1.4426950408889634
SUBLANES = 8
BF16_SUBLANES = 16
MXU_DIM = 256
VT_ROWS = HEAD_DIM + BF16_SUBLANES
VMEM_LIMIT_BYTES = 56 * 1024 * 1024

F32 = jnp.float32
BF16 = jnp.bfloat16
NT_DIMS = (((1,), (1,)), ((), ()))


def _sigmoid(z):
    return 1.0 / (1.0 + jnp.exp2(z * (-LOG2E)))


def _sqrt_nonneg(v):
    return jnp.where(v > 0.0, v * lax.rsqrt(v), 0.0)


def _norm_modulate(x, norm_g, shift, scale):
    ms = jnp.mean(x * x, axis=-1, keepdims=True)
    return x * lax.rsqrt(ms + EPS) * (norm_g * (1.0 + scale)) + shift


def _mod_kernel(c_ref, w_ref, b_ref, o_ref):
    c = c_ref[...]
    cs = (c * _sigmoid(c)).astype(BF16)
    w = w_ref[0].astype(BF16)
    o_ref[0] = jnp.dot(cs, w, preferred_element_type=F32) + b_ref[0]


def _modulation(c, w, b, tn):
    n_layers, d, n = w.shape
    bsz = c.shape[0]
    return pl.pallas_call(
        _mod_kernel,
        grid=(n_layers, n // tn),
        in_specs=[
            pl.BlockSpec((bsz, d), lambda l, j: (0, 0)),
            pl.BlockSpec((1, d, tn), lambda l, j: (l, 0, j)),
            pl.BlockSpec((1, 1, tn), lambda l, j: (l, 0, j)),
        ],
        out_specs=pl.BlockSpec((1, bsz, tn), lambda l, j: (l, 0, j)),
        out_shape=jax.ShapeDtypeStruct((n_layers, bsz, n), F32),
        compiler_params=pltpu.CompilerParams(
            dimension_semantics=("arbitrary", "arbitrary"),
            vmem_limit_bytes=VMEM_LIMIT_BYTES),
        name="adaln_mod",
    )(c, w, b.reshape(n_layers, 1, n))


def _tile_copies(hbm_ref, buf, sem, tile, slot, to_hbm):
    t_len, bsz, _ = buf.shape[1:]
    copies = []
    for b in range(bsz):
        hbm = hbm_ref.at[b, pl.ds(tile * t_len, t_len), :]
        vmem = buf.at[slot, :, b, :]
        src, dst = (vmem, hbm) if to_hbm else (hbm, vmem)
        copies.append(pltpu.make_async_copy(src, dst, sem.at[slot]))
    return copies


def _rglru_kernel(x_hbm, mod_ref, ng_ref, win_ref, cw_ref, cb_ref, wax_ref, ba_ref,
                  bx_ref, lam_ref, wout_ref, o_hbm,
                  xbuf, obuf, in_sem, out_sem, hs_s, y_s, utail, hstate):
    _, t_len, bsz, d = xbuf.shape
    n_layers, n_heads, rb, _ = wax_ref.shape
    conv_w = cw_ref.shape[1]
    i = pl.program_id(0)
    n_tiles = x_hbm.shape[1] // t_len
    slot = i & 1

    @pl.when(i == 0)
    def _():
        for c in _tile_copies(x_hbm, xbuf, in_sem, 0, 0, to_hbm=False):
            c.start()
        utail[...] = jnp.zeros_like(utail)
        hstate[...] = jnp.zeros_like(hstate)

    @pl.when(i + 1 < n_tiles)
    def _():
        for c in _tile_copies(x_hbm, xbuf, in_sem, i + 1, 1 - slot, to_hbm=False):
            c.start()

    @pl.when(i >= 2)
    def _():
        for c in _tile_copies(o_hbm, obuf, out_sem, i - 2, slot, to_hbm=True):
            c.wait()

    for c in _tile_copies(x_hbm, xbuf, in_sem, i, slot, to_hbm=False):
        c.wait()

    rows = t_len * bsz

    def layer(lyr, x3):
        mod = mod_ref[lyr]
        shift, scale, gate = mod[:, :d], mod[:, d:2 * d], mod[:, 2 * d:]
        lam = lam_ref[lyr]
        softplus_neg_lam = jnp.maximum(-lam, 0.0) + jnp.log1p(jnp.exp(-jnp.abs(lam)))
        log2_a_per_r = (-RG_C * LOG2E) * softplus_neg_lam

        ms = jnp.mean(x3 * x3, axis=-1, keepdims=True)
        h3 = x3 * lax.rsqrt(ms + EPS) * (ng_ref[lyr] * (1.0 + scale)) + shift
        hb = h3.reshape(rows, d).astype(BF16)
        u3 = jnp.dot(hb, win_ref[lyr, :, :d], preferred_element_type=F32).reshape(t_len, bsz, d)

        upad = jnp.concatenate([utail[lyr], u3], axis=0)
        utail[lyr] = u3[t_len - (conv_w - 1):]
        uc3 = cb_ref[lyr] + cw_ref[lyr, conv_w - 1:conv_w, :] * u3
        for k in range(conv_w - 1):
            uc3 = uc3 + cw_ref[lyr, k:k + 1, :] * upad[k:k + t_len]
        uc = uc3.reshape(rows, d)

        for hh in range(n_heads):
            sl = slice(hh * rb, (hh + 1) * rb)
            uch = uc[:, sl]
            z = jnp.dot(uch.astype(BF16), wax_ref[lyr, hh], preferred_element_type=F32)
            r = _sigmoid(z[:, :rb] + ba_ref[lyr, :, sl])
            gi = _sigmoid(z[:, rb:] + bx_ref[lyr, :, sl])
            a = jnp.exp2(r * log2_a_per_r[:, sl])
            b_in = _sqrt_nonneg(1.0 - a * a) * (gi * uch)
            a3 = a.reshape(t_len, bsz, rb)
            b3 = b_in.reshape(t_len, bsz, rb)
            h_run = hstate[lyr, :, sl]
            for t in range(t_len):
                h_run = a3[t] * h_run + b3[t]
                hs_s[t, :, sl] = h_run
            hstate[lyr, :, sl] = h_run
            gpath = jnp.dot(hb, win_ref[lyr, :, d + hh * rb:d + (hh + 1) * rb],
                            preferred_element_type=F32)
            y = hs_s[:, :, sl].reshape(rows, rb) * (gpath * _sigmoid(gpath))
            y_s[:, sl] = y.astype(BF16)

        proj = jnp.dot(y_s[...], wout_ref[lyr], preferred_element_type=F32)
        return x3 + gate * proj.reshape(t_len, bsz, d)

    x3 = xbuf[slot]
    for lyr in range(n_layers):
        x3 = layer(lyr, x3)
    obuf[slot] = x3

    for c in _tile_copies(o_hbm, obuf, out_sem, i, slot, to_hbm=True):
        c.start()

    @pl.when(i == n_tiles - 1)
    def _():
        if n_tiles >= 2:
            for c in _tile_copies(o_hbm, obuf, out_sem, i - 1, 1 - slot, to_hbm=True):
                c.wait()
        for c in _tile_copies(o_hbm, obuf, out_sem, i, slot, to_hbm=True):
            c.wait()


def _rglru_stack(x, mod, norm_g, w_in, conv_w, conv_b, w_ax, b_a, b_x, lam, w_out, t_len):
    bsz, seq, d = x.shape
    n_conv = conv_w.shape[1]
    assert bsz == SUBLANES and seq % t_len == 0 and t_len >= n_conv - 1
    n_layers = w_in.shape[0]
    full = lambda a: pl.BlockSpec(a.shape, lambda i: (0,) * a.ndim)
    return pl.pallas_call(
        _rglru_kernel,
        grid=(seq // t_len,),
        in_specs=[pl.BlockSpec(memory_space=pl.ANY)] + [
            full(a) for a in (mod, norm_g, w_in, conv_w, conv_b, w_ax, b_a, b_x, lam, w_out)],
        out_specs=pl.BlockSpec(memory_space=pl.ANY),
        out_shape=jax.ShapeDtypeStruct(x.shape, F32),
        scratch_shapes=[
            pltpu.VMEM((2, t_len, bsz, d), F32),
            pltpu.VMEM((2, t_len, bsz, d), F32),
            pltpu.SemaphoreType.DMA((2,)),
            pltpu.SemaphoreType.DMA((2,)),
            pltpu.VMEM((t_len, bsz, d), F32),
            pltpu.VMEM((t_len * bsz, d), BF16),
            pltpu.VMEM((n_layers, n_conv - 1, bsz, d), F32),
            pltpu.VMEM((n_layers, bsz, d), F32),
        ],
        compiler_params=pltpu.CompilerParams(
            dimension_semantics=("arbitrary",),
            vmem_limit_bytes=VMEM_LIMIT_BYTES),
        name="rglru_stack",
    )(x, mod, norm_g, w_in, conv_w, conv_b, w_ax, b_a, b_x, lam, w_out)


def _kv_kernel(x_ref, mod_ref, ng_ref, wk_ref, wvt_ref, k_ref, vt_ref, km_ref):
    d = x_ref.shape[2]
    n_heads = k_ref.shape[1]
    mod = mod_ref[0]
    h = _norm_modulate(x_ref[0], ng_ref[...], mod[:, :d], mod[:, d:]).astype(BF16)
    k = jnp.dot(h, wk_ref[...], preferred_element_type=F32)
    vt = lax.dot_general(wvt_ref[...], h, NT_DIMS, preferred_element_type=F32)
    for j in range(km_ref.shape[1]):
        km_ref[0, j] = jnp.mean(k[j * MOBA_BLOCK:(j + 1) * MOBA_BLOCK], axis=0, keepdims=True)
    pad_row = lax.broadcasted_iota(jnp.int32, (VT_ROWS - HEAD_DIM, vt.shape[1]), 0)
    ones_then_zeros = jnp.where(pad_row == 0, 1.0, 0.0).astype(BF16)
    for hd in range(n_heads):
        sl = slice(hd * HEAD_DIM, (hd + 1) * HEAD_DIM)
        k_ref[0, hd] = k[:, sl].astype(BF16)
        vt_ref[0, hd, :HEAD_DIM, :] = vt[sl, :].astype(BF16)
        vt_ref[0, hd, HEAD_DIM:, :] = ones_then_zeros


def _shared_kv(x, mod, norm_g, w_k, w_v_t, tile):
    bsz, seq, d = x.shape
    d_att = w_k.shape[1]
    n_heads = d_att // HEAD_DIM
    n_blk = seq // MOBA_BLOCK
    blk_per_tile = tile // MOBA_BLOCK
    assert tile % MOBA_BLOCK == 0 and seq % tile == 0
    return pl.pallas_call(
        _kv_kernel,
        grid=(bsz, seq // tile),
        in_specs=[
            pl.BlockSpec((1, tile, d), lambda b, i: (b, i, 0)),
            pl.BlockSpec((1, 1, 2 * d), lambda b, i: (b, 0, 0)),
            pl.BlockSpec((1, d), lambda b, i: (0, 0)),
            pl.BlockSpec((d, d_att), lambda b, i: (0, 0)),
            pl.BlockSpec((d_att, d), lambda b, i: (0, 0)),
        ],
        out_specs=[
            pl.BlockSpec((1, n_heads, tile, HEAD_DIM), lambda b, i: (b, 0, i, 0)),
            pl.BlockSpec((1, n_heads, VT_ROWS, tile), lambda b, i: (b, 0, 0, i)),
            pl.BlockSpec((1, blk_per_tile, 1, d_att), lambda b, i: (b, i, 0, 0)),
        ],
        out_shape=[jax.ShapeDtypeStruct((bsz, n_heads, seq, HEAD_DIM), BF16),
                   jax.ShapeDtypeStruct((bsz, n_heads, VT_ROWS, seq), BF16),
                   jax.ShapeDtypeStruct((bsz, n_blk, 1, d_att), F32)],
        compiler_params=pltpu.CompilerParams(
            dimension_semantics=("arbitrary", "arbitrary"),
            vmem_limit_bytes=VMEM_LIMIT_BYTES),
        name="shared_kv",
    )(x, mod, norm_g, w_k, w_v_t)


def _select_blocks(gate_t, n_past, sel_s):
    blk = lax.broadcasted_iota(jnp.int32, gate_t.shape, 0)
    g = jnp.where(blk < n_past, gate_t, NEG_INF)
    sel_s[...] = g
    rank = jnp.zeros(gate_t.shape, jnp.int32)
    for j in range(n_past):
        gj = jnp.broadcast_to(sel_s[j:j + 1, :], gate_t.shape)
        ahead = (gj > g) | ((gj == g) & (blk > j))
        rank = rank + ahead.astype(jnp.int32)
    sel_s[...] = jnp.where((rank < MOBA_TOPK) & (blk < n_past), 0.0, NEG_INF)


def _fold_rows(a, op):
    return op(a.reshape(a.shape[0] // SUBLANES, SUBLANES, a.shape[1]), axis=0)


MAX_HEADS_PER_STAGE = 8


def _heads_per_stage(n_past, slot_rows):
    per_stage = 1
    while (per_stage < MAX_HEADS_PER_STAGE
           and 2 * per_stage * (n_past + 1) * MOBA_BLOCK <= slot_rows):
        per_stage *= 2
    return per_stage


def _head_scores(hd, slot, lane, n_past, q_s, k_ref, km_ref, sel_s, sc_s, m_s):
    qf = q_s[hd]
    qh = (qf * (HEAD_DIM ** -0.5 * LOG2E)).astype(BF16)
    masked = n_past > MOBA_TOPK
    if masked:
        gate_t = lax.dot_general(km_ref[0, hd].astype(BF16), qf.astype(BF16), NT_DIMS,
                                 preferred_element_type=F32)
        sel_s = sel_s.at[lane]
        _select_blocks(gate_t, n_past, sel_s)
    n_keys = (n_past + 1) * MOBA_BLOCK
    s_all = lax.dot_general(k_ref[0, hd, 0:n_keys, :], qh, NT_DIMS,
                            preferred_element_type=F32)
    m8 = None
    for j in range(n_past + 1):
        rows = slice(j * MOBA_BLOCK, (j + 1) * MOBA_BLOCK)
        s = s_all[rows]
        if j == n_past:
            key = lax.broadcasted_iota(jnp.int32, s.shape, 0)
            qry = lax.broadcasted_iota(jnp.int32, s.shape, 1)
            s = jnp.where(key <= qry, s, NEG_INF)
        elif masked:
            s = s + sel_s[j:j + 1, :]
        sc_s[slot, lane * n_keys + j * MOBA_BLOCK:lane * n_keys + (j + 1) * MOBA_BLOCK] = s
        smax = _fold_rows(s, jnp.max)
        m8 = smax if m8 is None else jnp.maximum(m8, smax)
    m_s[slot, lane] = m8


def _head_output(hd, slot, lane, n_past, vt_ref, sc_s, m_s, o_s):
    n_keys = (n_past + 1) * MOBA_BLOCK
    m = jnp.max(m_s[slot, lane], axis=0, keepdims=True)
    p = jnp.exp2(sc_s[slot, lane * n_keys:(lane + 1) * n_keys] - m).astype(BF16)
    o_t = jnp.dot(vt_ref[0, hd, :, 0:n_keys], p, preferred_element_type=F32)
    o_s[hd] = o_t[:HEAD_DIM] * (1.0 / o_t[HEAD_DIM:HEAD_DIM + 1])


def _moba_kernel(x_ref, mod_ref, ng_ref, win_ref, k_ref, vt_ref, km_ref, wout_ref, fg_ref,
                 o_ref, x_s, hb_s, q_s, g_s, o_s, sel_s, sc_s, m_s):
    d = x_ref.shape[2]
    n_layers = win_ref.shape[0]
    n_heads = k_ref.shape[1]
    n_blk = km_ref.shape[2]
    d_att = n_heads * HEAD_DIM
    qb = pl.program_id(1)
    scores = functools.partial(_head_scores, q_s=q_s, k_ref=k_ref, km_ref=km_ref, sel_s=sel_s,
                               sc_s=sc_s, m_s=m_s)
    output = functools.partial(_head_output, vt_ref=vt_ref, sc_s=sc_s, m_s=m_s, o_s=o_s)

    def project_q(lyr):
        mod = mod_ref[0, pl.ds(lyr, 1), :]
        hb = _norm_modulate(x_s[...], ng_ref[lyr], mod[:, :d], mod[:, d:2 * d]).astype(BF16)
        hb_s[...] = hb
        q = jnp.dot(hb, win_ref[lyr, :, :d_att], preferred_element_type=F32)
        for hd in range(n_heads):
            q_s[hd] = q[:, hd * HEAD_DIM:(hd + 1) * HEAD_DIM]

    def attend_and_project(n_past, lyr):
        per_stage = _heads_per_stage(n_past, sc_s.shape[1])
        n_groups = n_heads // per_stage
        assert sc_s.shape[0] >= min(n_groups, 2)

        def stage(fn, group):
            for lane in range(per_stage):
                fn(group * per_stage + lane, group & 1, lane, n_past)

        stage(scores, 0)
        g_s[...] = jnp.dot(hb_s[...], win_ref[lyr, :, d_att:], preferred_element_type=F32)

        def step(group, carry):
            stage(output, group - 1)
            stage(scores, group)
            return carry

        lax.fori_loop(1, n_groups, step, 0)
        stage(output, n_groups - 1)

        ys = []
        for hd in range(n_heads):
            gp = g_s[:, hd * HEAD_DIM:(hd + 1) * HEAD_DIM]
            ys.append((o_s[hd].T * (gp * _sigmoid(gp))).astype(BF16))
        early = (n_heads - 1) * HEAD_DIM // MXU_DIM * MXU_DIM
        proj = jnp.dot(jnp.concatenate(ys[:early // HEAD_DIM], axis=1), wout_ref[lyr, :early, :],
                       preferred_element_type=F32)
        proj = proj + jnp.dot(jnp.concatenate(ys[early // HEAD_DIM:], axis=1),
                              wout_ref[lyr, early:, :], preferred_element_type=F32)
        gate = mod_ref[0, pl.ds(lyr, 1), 2 * d:]
        x_s[...] = x_s[...] + gate * proj

    x_s[...] = x_ref[0]

    def one_layer(lyr, carry):
        project_q(lyr)
        for n_past in range(n_blk):
            pl.when(qb == n_past)(functools.partial(attend_and_project, n_past, lyr))
        return carry

    lax.fori_loop(0, n_layers, one_layer, 0)
    out = x_s[...]
    ms = jnp.mean(out * out, axis=-1, keepdims=True)
    o_ref[0] = out * lax.rsqrt(ms + EPS) * fg_ref[...]


def _moba_stack(x, mod, norm_g, w_in, k, v_t, k_mean, w_out, final_g):
    bsz, seq, d = x.shape
    n_layers = w_in.shape[0]
    n_heads = k.shape[1]
    d_att = n_heads * HEAD_DIM
    n_blk = seq // MOBA_BLOCK
    full = lambda shape: pl.BlockSpec(shape, lambda b, i: (0,) * len(shape))
    per_batch = lambda shape, **kw: pl.BlockSpec(
        shape, lambda b, i: (b,) + (0,) * (len(shape) - 1), **kw)
    return pl.pallas_call(
        _moba_kernel,
        grid=(bsz, n_blk),
        in_specs=[
            pl.BlockSpec((1, MOBA_BLOCK, d), lambda b, i: (b, i, 0)),
            per_batch((1, n_layers, 3 * d)),
            full((n_layers, 1, d)),
            full((n_layers, d, 2 * d_att)),
            per_batch((1, n_heads, seq, HEAD_DIM), pipeline_mode=pl.Buffered(1)),
            per_batch((1, n_heads, VT_ROWS, seq), pipeline_mode=pl.Buffered(1)),
            per_batch((1, n_heads, n_blk, HEAD_DIM)),
            full((n_layers, d_att, d)),
            full((1, d)),
        ],
        out_specs=pl.BlockSpec((1, MOBA_BLOCK, d), lambda b, i: (b, i, 0)),
        out_shape=jax.ShapeDtypeStruct(x.shape, F32),
        scratch_shapes=[
            pltpu.VMEM((MOBA_BLOCK, d), F32),
            pltpu.VMEM((MOBA_BLOCK, d), BF16),
            pltpu.VMEM((n_heads, MOBA_BLOCK, HEAD_DIM), F32),
            pltpu.VMEM((MOBA_BLOCK, d_att), F32),
            pltpu.VMEM((n_heads, HEAD_DIM, MOBA_BLOCK), F32),
            pltpu.VMEM((MAX_HEADS_PER_STAGE, n_blk, MOBA_BLOCK), F32),
            pltpu.VMEM((1, n_heads * seq, MOBA_BLOCK), F32),
            pltpu.VMEM((2, MAX_HEADS_PER_STAGE, SUBLANES, MOBA_BLOCK), F32),
        ],
        compiler_params=pltpu.CompilerParams(
            dimension_semantics=("arbitrary", "arbitrary"),
            vmem_limit_bytes=VMEM_LIMIT_BYTES),
        name="moba_stack",
    )(x, mod, norm_g, w_in, k, v_t, k_mean, w_out, final_g)


def kernel(x, c, mod_w, mod_b, norm_g, rg_w_in, rg_conv_w, rg_conv_b, rg_w_a, rg_b_a, rg_w_x,
           rg_b_x, rg_lambda, rg_w_out, kv_norm_g, kv_mod_w, kv_mod_b, w_kv, att_w_in,
           att_w_out, final_norm_g):
    bsz, seq, d = x.shape
    depth = mod_w.shape[0]
    n_a = rg_w_in.shape[0]
    n_b = att_w_in.shape[0]
    assert depth == n_a + n_b and seq % MOBA_BLOCK == 0
    d_att = w_kv.shape[1] // 2
    n_heads = d_att // HEAD_DIM
    n_blk = seq // MOBA_BLOCK

    mod = _modulation(c, mod_w, mod_b, tn=d * 3 // 2)
    kv_mod = _modulation(c, kv_mod_w[None], kv_mod_b[None], tn=d)

    row = lambda p: p.reshape(1, -1)
    rows = lambda p: p[:, None, :]
    w_ax = jnp.concatenate([rg_w_a, rg_w_x], axis=-1).astype(BF16)
    x = _rglru_stack(
        x, mod[:n_a], rows(norm_g[:n_a]), rg_w_in.astype(BF16), rg_conv_w, rows(rg_conv_b), w_ax,
        rows(rg_b_a), rows(rg_b_x), rows(rg_lambda), rg_w_out.astype(BF16), t_len=128)

    k, v_t, k_mean = _shared_kv(x, kv_mod[0][:, None, :], row(kv_norm_g),
                                w_kv[:, :d_att].astype(BF16), w_kv[:, d_att:].T.astype(BF16),
                                tile=4 * MOBA_BLOCK)
    k_mean = k_mean.reshape(bsz, n_blk, n_heads, HEAD_DIM).transpose(0, 2, 1, 3)

    return _moba_stack(
        x, mod[n_a:].transpose(1, 0, 2), norm_g[n_a:, None, :], att_w_in.astype(BF16), k, v_t,
        k_mean, att_w_out.astype(BF16), row(final_norm_g))
```

```python
import functools

import jax
import jax.numpy as jnp
from jax import lax
from jax.experimental import pallas as pl
from jax.experimental.pallas import tpu as pltpu

EPS = 1e-6
RG_C = 8.0
HEAD_DIM = 128
MOBA_BLOCK = 256
MOBA_TOPK = 3
NEG_INF = -1e30
LOG2E = 1.4426950408889634
SUBLANES = 8
BF16_SUBLANES = 16
MXU_DIM = 256
VT_ROWS = HEAD_DIM + BF16_SUBLANES
VMEM_LIMIT_BYTES = 56 * 1024 * 1024

F32 = jnp.float32
BF16 = jnp.bfloat16
NT_DIMS = (((1,), (1,)), ((), ()))


def _sigmoid(z):
    return 1.0 / (1.0 + jnp.exp2(z * (-LOG2E)))


def _sqrt_nonneg(v):
    return jnp.where(v > 0.0, v * lax.rsqrt(v), 0.0)


def _norm_modulate(x, norm_g, shift, scale):
    ms = jnp.mean(x * x, axis=-1, keepdims=True)
    return x * lax.rsqrt(ms + EPS) * (norm_g * (1.0 + scale)) + shift


def _mod_kernel(c_ref, w_ref, b_ref, o_ref):
    c = c_ref[...]
    cs = (c * _sigmoid(c)).astype(BF16)
    w = w_ref[0].astype(BF16)
    o_ref[0] = jnp.dot(cs, w, preferred_element_type=F32) + b_ref[0]


def _modulation(c, w, b, tn):
    n_layers, d, n = w.shape
    bsz = c.shape[0]
    return pl.pallas_call(
        _mod_kernel,
        grid=(n_layers, n // tn),
        in_specs=[
            pl.BlockSpec((bsz, d), lambda l, j: (0, 0)),
            pl.BlockSpec((1, d, tn), lambda l, j: (l, 0, j)),
            pl.BlockSpec((1, 1, tn), lambda l, j: (l, 0, j)),
        ],
        out_specs=pl.BlockSpec((1, bsz, tn), lambda l, j: (l, 0, j)),
        out_shape=jax.ShapeDtypeStruct((n_layers, bsz, n), F32),
        compiler_params=pltpu.CompilerParams(
            dimension_semantics=("arbitrary", "arbitrary"),
            vmem_limit_bytes=VMEM_LIMIT_BYTES),
        name="adaln_mod",
    )(c, w, b.reshape(n_layers, 1, n))


def _tile_copies(hbm_ref, buf, sem, tile, slot, to_hbm):
    t_len, bsz, _ = buf.shape[1:]
    copies = []
    for b in range(bsz):
        hbm = hbm_ref.at[b, pl.ds(tile * t_len, t_len), :]
        vmem = buf.at[slot, :, b, :]
        src, dst = (vmem, hbm) if to_hbm else (hbm, vmem)
        copies.append(pltpu.make_async_copy(src, dst, sem.at[slot]))
    return copies


def _rglru_kernel(x_hbm, mod_ref, ng_ref, win_ref, cw_ref, cb_ref, wax_ref, ba_ref,
                  bx_ref, lam_ref, wout_ref, o_hbm,
                  xbuf, obuf, in_sem, out_sem, hs_s, y_s, utail, hstate):
    _, t_len, bsz, d = xbuf.shape
    n_layers, n_heads, rb, _ = wax_ref.shape
    conv_w = cw_ref.shape[1]
    i = pl.program_id(0)
    n_tiles = x_hbm.shape[1] // t_len
    slot = i & 1

    @pl.when(i == 0)
    def _():
        for c in _tile_copies(x_hbm, xbuf, in_sem, 0, 0, to_hbm=False):
            c.start()
        utail[...] = jnp.zeros_like(utail)
        hstate[...] = jnp.zeros_like(hstate)

    @pl.when(i + 1 < n_tiles)
    def _():
        for c in _tile_copies(x_hbm, xbuf, in_sem, i + 1, 1 - slot, to_hbm=False):
            c.start()

    @pl.when(i >= 2)
    def _():
        for c in _tile_copies(o_hbm, obuf, out_sem, i - 2, slot, to_hbm=True):
            c.wait()

    for c in _tile_copies(x_hbm, xbuf, in_sem, i, slot, to_hbm=False):
        c.wait()

    rows = t_len * bsz

    def layer(lyr, x3):
        mod = mod_ref[lyr]
        shift, scale, gate = mod[:, :d], mod[:, d:2 * d], mod[:, 2 * d:]
        lam = lam_ref[lyr]
        softplus_neg_lam = jnp.maximum(-lam, 0.0) + jnp.log1p(jnp.exp(-jnp.abs(lam)))
        log2_a_per_r = (-RG_C * LOG2E) * softplus_neg_lam

        ms = jnp.mean(x3 * x3, axis=-1, keepdims=True)
        h3 = x3 * lax.rsqrt(ms + EPS) * (ng_ref[lyr] * (1.0 + scale)) + shift
        hb = h3.reshape(rows, d).astype(BF16)
        u3 = jnp.dot(hb, win_ref[lyr, :, :d], preferred_element_type=F32).reshape(t_len, bsz, d)

        upad = jnp.concatenate([utail[lyr], u3], axis=0)
        utail[lyr] = u3[t_len - (conv_w - 1):]
        uc3 = cb_ref[lyr] + cw_ref[lyr, conv_w - 1:conv_w, :] * u3
        for k in range(conv_w - 1):
            uc3 = uc3 + cw_ref[lyr, k:k + 1, :] * upad[k:k + t_len]
        uc = uc3.reshape(rows, d)

        for hh in range(n_heads):
            sl = slice(hh * rb, (hh + 1) * rb)
            uch = uc[:, sl]
            z = jnp.dot(uch.astype(BF16), wax_ref[lyr, hh], preferred_element_type=F32)
            r = _sigmoid(z[:, :rb] + ba_ref[lyr, :, sl])
            gi = _sigmoid(z[:, rb:] + bx_ref[lyr, :, sl])
            a = jnp.exp2(r * log2_a_per_r[:, sl])
            b_in = _sqrt_nonneg(1.0 - a * a) * (gi * uch)
            a3 = a.reshape(t_len, bsz, rb)
            b3 = b_in.reshape(t_len, bsz, rb)
            h_run = hstate[lyr, :, sl]
            for t in range(t_len):
                h_run = a3[t] * h_run + b3[t]
                hs_s[t, :, sl] = h_run
            hstate[lyr, :, sl] = h_run
            gpath = jnp.dot(hb, win_ref[lyr, :, d + hh * rb:d + (hh + 1) * rb],
                            preferred_element_type=F32)
            y = hs_s[:, :, sl].reshape(rows, rb) * (gpath * _sigmoid(gpath))
            y_s[:, sl] = y.astype(BF16)

        proj = jnp.dot(y_s[...], wout_ref[lyr], preferred_element_type=F32)
        return x3 + gate * proj.reshape(t_len, bsz, d)

    x3 = xbuf[slot]
    for lyr in range(n_layers):
        x3 = layer(lyr, x3)
    obuf[slot] = x3

    for c in _tile_copies(o_hbm, obuf, out_sem, i, slot, to_hbm=True):
        c.start()

    @pl.when(i == n_tiles - 1)
    def _():
        if n_tiles >= 2:
            for c in _tile_copies(o_hbm, obuf, out_sem, i - 1, 1 - slot, to_hbm=True):
                c.wait()
        for c in _tile_copies(o_hbm, obuf, out_sem, i, slot, to_hbm=True):
            c.wait()


def _rglru_stack(x, mod, norm_g, w_in, conv_w, conv_b, w_ax, b_a, b_x, lam, w_out, t_len):
    bsz, seq, d = x.shape
    n_conv = conv_w.shape[1]
    assert bsz == SUBLANES and seq % t_len == 0 and t_len >= n_conv - 1
    n_layers = w_in.shape[0]
    full = lambda a: pl.BlockSpec(a.shape, lambda i: (0,) * a.ndim)
    return pl.pallas_call(
        _rglru_kernel,
        grid=(seq // t_len,),
        in_specs=[pl.BlockSpec(memory_space=pl.ANY)] + [
            full(a) for a in (mod, norm_g, w_in, conv_w, conv_b, w_ax, b_a, b_x, lam, w_out)],
        out_specs=pl.BlockSpec(memory_space=pl.ANY),
        out_shape=jax.ShapeDtypeStruct(x.shape, F32),
        scratch_shapes=[
            pltpu.VMEM((2, t_len, bsz, d), F32),
            pltpu.VMEM((2, t_len, bsz, d), F32),
            pltpu.SemaphoreType.DMA((2,)),
            pltpu.SemaphoreType.DMA((2,)),
            pltpu.VMEM((t_len, bsz, d), F32),
            pltpu.VMEM((t_len * bsz, d), BF16),
            pltpu.VMEM((n_layers, n_conv - 1, bsz, d), F32),
            pltpu.VMEM((n_layers, bsz, d), F32),
        ],
        compiler_params=pltpu.CompilerParams(
            dimension_semantics=("arbitrary",),
            vmem_limit_bytes=VMEM_LIMIT_BYTES),
        name="rglru_stack",
    )(x, mod, norm_g, w_in, conv_w, conv_b, w_ax, b_a, b_x, lam, w_out)


def _kv_kernel(x_ref, mod_ref, ng_ref, wk_ref, wvt_ref, k_ref, vt_ref, km_ref):
    d = x_ref.shape[2]
    n_heads = k_ref.shape[1]
    mod = mod_ref[0]
    h = _norm_modulate(x_ref[0], ng_ref[...], mod[:, :d], mod[:, d:]).astype(BF16)
    k = jnp.dot(h, wk_ref[...], preferred_element_type=F32)
    vt = lax.dot_general(wvt_ref[...], h, NT_DIMS, preferred_element_type=F32)
    for j in range(km_ref.shape[1]):
        km_ref[0, j] = jnp.mean(k[j * MOBA_BLOCK:(j + 1) * MOBA_BLOCK], axis=0, keepdims=True)
    pad_row = lax.broadcasted_iota(jnp.int32, (VT_ROWS - HEAD_DIM, vt.shape[1]), 0)
    ones_then_zeros = jnp.where(pad_row == 0, 1.0, 0.0).astype(BF16)
    for hd in range(n_heads):
        sl = slice(hd * HEAD_DIM, (hd + 1) * HEAD_DIM)
        k_ref[0, hd] = k[:, sl].astype(BF16)
        vt_ref[0, hd, :HEAD_DIM, :] = vt[sl, :].astype(BF16)
        vt_ref[0, hd, HEAD_DIM:, :] = ones_then_zeros


def _shared_kv(x, mod, norm_g, w_k, w_v_t, tile):
    bsz, seq, d = x.shape
    d_att = w_k.shape[1]
    n_heads = d_att // HEAD_DIM
    n_blk = seq // MOBA_BLOCK
    blk_per_tile = tile // MOBA_BLOCK
    assert tile % MOBA_BLOCK == 0 and seq % tile == 0
    return pl.pallas_call(
        _kv_kernel,
        grid=(bsz, seq // tile),
        in_specs=[
            pl.BlockSpec((1, tile, d), lambda b, i: (b, i, 0)),
            pl.BlockSpec((1, 1, 2 * d), lambda b, i: (b, 0, 0)),
            pl.BlockSpec((1, d), lambda b, i: (0, 0)),
            pl.BlockSpec((d, d_att), lambda b, i: (0, 0)),
            pl.BlockSpec((d_att, d), lambda b, i: (0, 0)),
        ],
        out_specs=[
            pl.BlockSpec((1, n_heads, tile, HEAD_DIM), lambda b, i: (b, 0, i, 0)),
            pl.BlockSpec((1, n_heads, VT_ROWS, tile), lambda b, i: (b, 0, 0, i)),
            pl.BlockSpec((1, blk_per_tile, 1, d_att), lambda b, i: (b, i, 0, 0)),
        ],
        out_shape=[jax.ShapeDtypeStruct((bsz, n_heads, seq, HEAD_DIM), BF16),
                   jax.ShapeDtypeStruct((bsz, n_heads, VT_ROWS, seq), BF16),
                   jax.ShapeDtypeStruct((bsz, n_blk, 1, d_att), F32)],
        compiler_params=pltpu.CompilerParams(
            dimension_semantics=("arbitrary", "arbitrary"),
            vmem_limit_bytes=VMEM_LIMIT_BYTES),
        name="shared_kv",
    )(x, mod, norm_g, w_k, w_v_t)


def _select_blocks(gate_t, n_past, sel_s):
    blk = lax.broadcasted_iota(jnp.int32, gate_t.shape, 0)
    g = jnp.where(blk < n_past, gate_t, NEG_INF)
    sel_s[...] = g
    rank = jnp.zeros(gate_t.shape, jnp.int32)
    for j in range(n_past):
        gj = jnp.broadcast_to(sel_s[j:j + 1, :], gate_t.shape)
        ahead = (gj > g) | ((gj == g) & (blk > j))
        rank = rank + ahead.astype(jnp.int32)
    sel_s[...] = jnp.where((rank < MOBA_TOPK) & (blk < n_past), 0.0, NEG_INF)


def _fold_rows(a, op):
    return op(a.reshape(a.shape[0] // SUBLANES, SUBLANES, a.shape[1]), axis=0)


MAX_HEADS_PER_STAGE = 8


def _heads_per_stage(n_past, slot_rows):
    per_stage = 1
    while (per_stage < MAX_HEADS_PER_STAGE
           and 2 * per_stage * (n_past + 1) * MOBA_BLOCK <= slot_rows):
        per_stage *= 2
    return per_stage


def _head_scores(hd, slot, lane, n_past, q_s, k_ref, km_ref, sel_s, sc_s, m_s):
    qf = q_s[hd]
    qh = (qf * (HEAD_DIM ** -0.5 * LOG2E)).astype(BF16)
    masked = n_past > MOBA_TOPK
    if masked:
        gate_t = lax.dot_general(km_ref[0, hd].astype(BF16), qf.astype(BF16), NT_DIMS,
                                 preferred_element_type=F32)
        sel_s = sel_s.at[lane]
        _select_blocks(gate_t, n_past, sel_s)
    n_keys = (n_past + 1) * MOBA_BLOCK
    s_all = lax.dot_general(k_ref[0, hd, 0:n_keys, :], qh, NT_DIMS,
                            preferred_element_type=F32)
    m8 = None
    for j in range(n_past + 1):
        rows = slice(j * MOBA_BLOCK, (j + 1) * MOBA_BLOCK)
        s = s_all[rows]
        if j == n_past:
            key = lax.broadcasted_iota(jnp.int32, s.shape, 0)
            qry = lax.broadcasted_iota(jnp.int32, s.shape, 1)
            s = jnp.where(key <= qry, s, NEG_INF)
        elif masked:
            s = s + sel_s[j:j + 1, :]
        sc_s[slot, lane * n_keys + j * MOBA_BLOCK:lane * n_keys + (j + 1) * MOBA_BLOCK] = s
        smax = _fold_rows(s, jnp.max)
        m8 = smax if m8 is None else jnp.maximum(m8, smax)
    m_s[slot, lane] = m8


def _head_output(hd, slot, lane, n_past, vt_ref, sc_s, m_s, o_s):
    n_keys = (n_past + 1) * MOBA_BLOCK
    m = jnp.max(m_s[slot, lane], axis=0, keepdims=True)
    p = jnp.exp2(sc_s[slot, lane * n_keys:(lane + 1) * n_keys] - m).astype(BF16)
    o_t = jnp.dot(vt_ref[0, hd, :, 0:n_keys], p, preferred_element_type=F32)
    o_s[hd] = o_t[:HEAD_DIM] * (1.0 / o_t[HEAD_DIM:HEAD_DIM + 1])


def _moba_kernel(x_ref, mod_ref, ng_ref, win_ref, k_ref, vt_ref, km_ref, wout_ref, fg_ref,
                 o_ref, x_s, hb_s, q_s, g_s, o_s, sel_s, sc_s, m_s):
    d = x_ref.shape[2]
    n_layers = win_ref.shape[0]
    n_heads = k_ref.shape[1]
    n_blk = km_ref.shape[2]
    d_att = n_heads * HEAD_DIM
    qb = pl.program_id(1)
    scores = functools.partial(_head_scores, q_s=q_s, k_ref=k_ref, km_ref=km_ref, sel_s=sel_s,
                               sc_s=sc_s, m_s=m_s)
    output = functools.partial(_head_output, vt_ref=vt_ref, sc_s=sc_s, m_s=m_s, o_s=o_s)

    def project_q(lyr):
        mod = mod_ref[0, pl.ds(lyr, 1), :]
        hb = _norm_modulate(x_s[...], ng_ref[lyr], mod[:, :d], mod[:, d:2 * d]).astype(BF16)
        hb_s[...] = hb
        q = jnp.dot(hb, win_ref[lyr, :, :d_att], preferred_element_type=F32)
        for hd in range(n_heads):
            q_s[hd] = q[:, hd * HEAD_DIM:(hd + 1) * HEAD_DIM]

    def attend_and_project(n_past, lyr):
        per_stage = _heads_per_stage(n_past, sc_s.shape[1])
        n_groups = n_heads // per_stage
        assert sc_s.shape[0] >= min(n_groups, 2)

        def stage(fn, group):
            for lane in range(per_stage):
                fn(group * per_stage + lane, group & 1, lane, n_past)

        stage(scores, 0)
        g_s[...] = jnp.dot(hb_s[...], win_ref[lyr, :, d_att:], preferred_element_type=F32)

        def step(group, carry):
            stage(output, group - 1)
            stage(scores, group)
            return carry

        lax.fori_loop(1, n_groups, step, 0)
        stage(output, n_groups - 1)

        ys = []
        for hd in range(n_heads):
            gp = g_s[:, hd * HEAD_DIM:(hd + 1) * HEAD_DIM]
            ys.append((o_s[hd].T * (gp * _sigmoid(gp))).astype(BF16))
        early = (n_heads - 1) * HEAD_DIM // MXU_DIM * MXU_DIM
        proj = jnp.dot(jnp.concatenate(ys[:early // HEAD_DIM], axis=1), wout_ref[lyr, :early, :],
                       preferred_element_type=F32)
        proj = proj + jnp.dot(jnp.concatenate(ys[early // HEAD_DIM:], axis=1),
                              wout_ref[lyr, early:, :], preferred_element_type=F32)
        gate = mod_ref[0, pl.ds(lyr, 1), 2 * d:]
        x_s[...] = x_s[...] + gate * proj

    x_s[...] = x_ref[0]

    def one_layer(lyr, carry):
        project_q(lyr)
        for n_past in range(n_blk):
            pl.when(qb == n_past)(functools.partial(attend_and_project, n_past, lyr))
        return carry

    lax.fori_loop(0, n_layers, one_layer, 0)
    out = x_s[...]
    ms = jnp.mean(out * out, axis=-1, keepdims=True)
    o_ref[0] = out * lax.rsqrt(ms + EPS) * fg_ref[...]


def _moba_stack(x, mod, norm_g, w_in, k, v_t, k_mean, w_out, final_g):
    bsz, seq, d = x.shape
    n_layers = w_in.shape[0]
    n_heads = k.shape[1]
    d_att = n_heads * HEAD_DIM
    n_blk = seq // MOBA_BLOCK
    full = lambda shape: pl.BlockSpec(shape, lambda b, i: (0,) * len(shape))
    per_batch = lambda shape, **kw: pl.BlockSpec(
        shape, lambda b, i: (b,) + (0,) * (len(shape) - 1), **kw)
    return pl.pallas_call(
        _moba_kernel,
        grid=(bsz, n_blk),
        in_specs=[
            pl.BlockSpec((1, MOBA_BLOCK, d), lambda b, i: (b, i, 0)),
            per_batch((1, n_layers, 3 * d)),
            full((n_layers, 1, d)),
            full((n_layers, d, 2 * d_att)),
            per_batch((1, n_heads, seq, HEAD_DIM)),
            per_batch((1, n_heads, VT_ROWS, seq), pipeline_mode=pl.Buffered(1)),
            per_batch((1, n_heads, n_blk, HEAD_DIM)),
            full((n_layers, d_att, d)),
            full((1, d)),
        ],
        out_specs=pl.BlockSpec((1, MOBA_BLOCK, d), lambda b, i: (b, i, 0)),
        out_shape=jax.ShapeDtypeStruct(x.shape, F32),
        scratch_shapes=[
            pltpu.VMEM((MOBA_BLOCK, d), F32),
            pltpu.VMEM((MOBA_BLOCK, d), BF16),
            pltpu.VMEM((n_heads, MOBA_BLOCK, HEAD_DIM), F32),
            pltpu.VMEM((MOBA_BLOCK, d_att), F32),
            pltpu.VMEM((n_heads, HEAD_DIM, MOBA_BLOCK), F32),
            pltpu.VMEM((MAX_HEADS_PER_STAGE, n_blk, MOBA_BLOCK), F32),
            pltpu.VMEM((1, n_heads * seq, MOBA_BLOCK), F32),
            pltpu.VMEM((2, MAX_HEADS_PER_STAGE, SUBLANES, MOBA_BLOCK), F32),
        ],
        compiler_params=pltpu.CompilerParams(
            dimension_semantics=("arbitrary", "arbitrary"),
            vmem_limit_bytes=VMEM_LIMIT_BYTES),
        name="moba_stack",
    )(x, mod, norm_g, w_in, k, v_t, k_mean, w_out, final_g)


def kernel(x, c, mod_w, mod_b, norm_g, rg_w_in, rg_conv_w, rg_conv_b, rg_w_a, rg_b_a, rg_w_x,
           rg_b_x, rg_lambda, rg_w_out, kv_norm_g, kv_mod_w, kv_mod_b, w_kv, att_w_in,
           att_w_out, final_norm_g):
    bsz, seq, d = x.shape
    depth = mod_w.shape[0]
    n_a = rg_w_in.shape[0]
    n_b = att_w_in.shape[0]
    assert depth == n_a + n_b and seq % MOBA_BLOCK == 0
    d_att = w_kv.shape[1] // 2
    n_heads = d_att // HEAD_DIM
    n_blk = seq // MOBA_BLOCK

    mod = _modulation(c, mod_w, mod_b, tn=d * 3 // 2)
    kv_mod = _modulation(c, kv_mod_w[None], kv_mod_b[None], tn=d)

    row = lambda p: p.reshape(1, -1)
    rows = lambda p: p[:, None, :]
    w_ax = jnp.concatenate([rg_w_a, rg_w_x], axis=-1).astype(BF16)
    x = _rglru_stack(
        x, mod[:n_a], rows(norm_g[:n_a]), rg_w_in.astype(BF16), rg_conv_w, rows(rg_conv_b), w_ax,
        rows(rg_b_a), rows(rg_b_x), rows(rg_lambda), rg_w_out.astype(BF16), t_len=128)

    w_kv_b = w_kv.astype(BF16)
    k, v_t, k_mean = _shared_kv(x, kv_mod[0][:, None, :], row(kv_norm_g),
                                w_kv_b[:, :d_att], w_kv_b[:, d_att:].T, tile=4 * MOBA_BLOCK)
    k_mean = k_mean.reshape(bsz, n_blk, n_heads, HEAD_DIM).transpose(0, 2, 1, 3)

    return _moba_stack(
        x, mod[n_a:].transpose(1, 0, 2), norm_g[n_a:, None, :], att_w_in.astype(BF16), k, v_t,
        k_mean, att_w_out.astype(BF16), row(final_norm_g))
```

```python
import functools

import jax
import jax.numpy as jnp
from jax import lax
from jax.experimental import pallas as pl
from jax.experimental.pallas import tpu as pltpu

EPS = 1e-6
RG_C = 8.0
HEAD_DIM = 128
MOBA_BLOCK = 256
MOBA_TOPK = 3
NEG_INF = -1e30
LOG2E = 1.4426950408889634
SUBLANES = 8
BF16_SUBLANES = 16
MXU_DIM = 256
VT_ROWS = HEAD_DIM + BF16_SUBLANES
VMEM_LIMIT_BYTES = 56 * 1024 * 1024

F32 = jnp.float32
BF16 = jnp.bfloat16
NT_DIMS = (((1,), (1,)), ((), ()))


def _sigmoid(z):
    return 1.0 / (1.0 + jnp.exp2(z * (-LOG2E)))


def _sqrt_nonneg(v):
    return jnp.where(v > 0.0, v * lax.rsqrt(v), 0.0)


def _norm_modulate(x, norm_g, shift, scale):
    ms = jnp.mean(x * x, axis=-1, keepdims=True)
    return x * lax.rsqrt(ms + EPS) * (norm_g * (1.0 + scale)) + shift


def _mod_kernel(c_ref, w_ref, b_ref, o_ref):
    c = c_ref[...]
    cs = (c * _sigmoid(c)).astype(BF16)
    w = w_ref[0].astype(BF16)
    o_ref[0] = jnp.dot(cs, w, preferred_element_type=F32) + b_ref[0]


def _modulation(c, w, b, tn):
    n_layers, d, n = w.shape
    bsz = c.shape[0]
    return pl.pallas_call(
        _mod_kernel,
        grid=(n_layers, n // tn),
        in_specs=[
            pl.BlockSpec((bsz, d), lambda l, j: (0, 0)),
            pl.BlockSpec((1, d, tn), lambda l, j: (l, 0, j)),
            pl.BlockSpec((1, 1, tn), lambda l, j: (l, 0, j)),
        ],
        out_specs=pl.BlockSpec((1, bsz, tn), lambda l, j: (l, 0, j)),
        out_shape=jax.ShapeDtypeStruct((n_layers, bsz, n), F32),
        compiler_params=pltpu.CompilerParams(
            dimension_semantics=("arbitrary", "arbitrary"),
            vmem_limit_bytes=VMEM_LIMIT_BYTES),
        name="adaln_mod",
    )(c, w, b.reshape(n_layers, 1, n))


def _tile_copies(hbm_ref, buf, sem, tile, slot, to_hbm):
    t_len, bsz, _ = buf.shape[1:]
    copies = []
    for b in range(bsz):
        hbm = hbm_ref.at[b, pl.ds(tile * t_len, t_len), :]
        vmem = buf.at[slot, :, b, :]
        src, dst = (vmem, hbm) if to_hbm else (hbm, vmem)
        copies.append(pltpu.make_async_copy(src, dst, sem.at[slot]))
    return copies


def _rglru_kernel(x_hbm, mod_ref, ng_ref, win_ref, cw_ref, cb_ref, wax_ref, ba_ref,
                  bx_ref, lam_ref, wout_ref, o_hbm,
                  xbuf, obuf, in_sem, out_sem, hs_s, y_s, utail, hstate):
    _, t_len, bsz, d = xbuf.shape
    n_layers, n_heads, rb, _ = wax_ref.shape
    conv_w = cw_ref.shape[1]
    i = pl.program_id(0)
    n_tiles = x_hbm.shape[1] // t_len
    slot = i & 1

    @pl.when(i == 0)
    def _():
        for c in _tile_copies(x_hbm, xbuf, in_sem, 0, 0, to_hbm=False):
            c.start()
        utail[...] = jnp.zeros_like(utail)
        hstate[...] = jnp.zeros_like(hstate)

    @pl.when(i + 1 < n_tiles)
    def _():
        for c in _tile_copies(x_hbm, xbuf, in_sem, i + 1, 1 - slot, to_hbm=False):
            c.start()

    @pl.when(i >= 2)
    def _():
        for c in _tile_copies(o_hbm, obuf, out_sem, i - 2, slot, to_hbm=True):
            c.wait()

    for c in _tile_copies(x_hbm, xbuf, in_sem, i, slot, to_hbm=False):
        c.wait()

    rows = t_len * bsz

    def layer(lyr, x3):
        mod = mod_ref[lyr]
        shift, scale, gate = mod[:, :d], mod[:, d:2 * d], mod[:, 2 * d:]
        lam = lam_ref[lyr]
        softplus_neg_lam = jnp.maximum(-lam, 0.0) + jnp.log1p(jnp.exp(-jnp.abs(lam)))
        log2_a_per_r = (-RG_C * LOG2E) * softplus_neg_lam

        ms = jnp.mean(x3 * x3, axis=-1, keepdims=True)
        h3 = x3 * lax.rsqrt(ms + EPS) * (ng_ref[lyr] * (1.0 + scale)) + shift
        hb = h3.reshape(rows, d).astype(BF16)
        u3 = jnp.dot(hb, win_ref[lyr, :, :d], preferred_element_type=F32).reshape(t_len, bsz, d)

        upad = jnp.concatenate([utail[lyr], u3], axis=0)
        utail[lyr] = u3[t_len - (conv_w - 1):]
        uc3 = cb_ref[lyr] + cw_ref[lyr, conv_w - 1:conv_w, :] * u3
        for k in range(conv_w - 1):
            uc3 = uc3 + cw_ref[lyr, k:k + 1, :] * upad[k:k + t_len]
        uc = uc3.reshape(rows, d)

        for hh in range(n_heads):
            sl = slice(hh * rb, (hh + 1) * rb)
            uch = uc[:, sl]
            z = jnp.dot(uch.astype(BF16), wax_ref[lyr, hh], preferred_element_type=F32)
            r = _sigmoid(z[:, :rb] + ba_ref[lyr, :, sl])
            gi = _sigmoid(z[:, rb:] + bx_ref[lyr, :, sl])
            a = jnp.exp2(r * log2_a_per_r[:, sl])
            b_in = _sqrt_nonneg(1.0 - a * a) * (gi * uch)
            a3 = a.reshape(t_len, bsz, rb)
            b3 = b_in.reshape(t_len, bsz, rb)
            h_run = hstate[lyr, :, sl]
            for t in range(t_len):
                h_run = a3[t] * h_run + b3[t]
                hs_s[t, :, sl] = h_run
            hstate[lyr, :, sl] = h_run
            gpath = jnp.dot(hb, win_ref[lyr, :, d + hh * rb:d + (hh + 1) * rb],
                            preferred_element_type=F32)
            y = hs_s[:, :, sl].reshape(rows, rb) * (gpath * _sigmoid(gpath))
            y_s[:, sl] = y.astype(BF16)

        proj = jnp.dot(y_s[...], wout_ref[lyr], preferred_element_type=F32)
        return x3 + gate * proj.reshape(t_len, bsz, d)

    x3 = xbuf[slot]
    for lyr in range(n_layers):
        x3 = layer(lyr, x3)
    obuf[slot] = x3

    for c in _tile_copies(o_hbm, obuf, out_sem, i, slot, to_hbm=True):
        c.start()

    @pl.when(i == n_tiles - 1)
    def _():
        if n_tiles >= 2:
            for c in _tile_copies(o_hbm, obuf, out_sem, i - 1, 1 - slot, to_hbm=True):
                c.wait()
        for c in _tile_copies(o_hbm, obuf, out_sem, i, slot, to_hbm=True):
            c.wait()


def _rglru_stack(x, mod, norm_g, w_in, conv_w, conv_b, w_ax, b_a, b_x, lam, w_out, t_len):
    bsz, seq, d = x.shape
    n_conv = conv_w.shape[1]
    assert bsz == SUBLANES and seq % t_len == 0 and t_len >= n_conv - 1
    n_layers = w_in.shape[0]
    full = lambda a: pl.BlockSpec(a.shape, lambda i: (0,) * a.ndim)
    return pl.pallas_call(
        _rglru_kernel,
        grid=(seq // t_len,),
        in_specs=[pl.BlockSpec(memory_space=pl.ANY)] + [
            full(a) for a in (mod, norm_g, w_in, conv_w, conv_b, w_ax, b_a, b_x, lam, w_out)],
        out_specs=pl.BlockSpec(memory_space=pl.ANY),
        out_shape=jax.ShapeDtypeStruct(x.shape, F32),
        scratch_shapes=[
            pltpu.VMEM((2, t_len, bsz, d), F32),
            pltpu.VMEM((2, t_len, bsz, d), F32),
            pltpu.SemaphoreType.DMA((2,)),
            pltpu.SemaphoreType.DMA((2,)),
            pltpu.VMEM((t_len, bsz, d), F32),
            pltpu.VMEM((t_len * bsz, d), BF16),
            pltpu.VMEM((n_layers, n_conv - 1, bsz, d), F32),
            pltpu.VMEM((n_layers, bsz, d), F32),
        ],
        compiler_params=pltpu.CompilerParams(
            dimension_semantics=("arbitrary",),
            vmem_limit_bytes=VMEM_LIMIT_BYTES),
        name="rglru_stack",
    )(x, mod, norm_g, w_in, conv_w, conv_b, w_ax, b_a, b_x, lam, w_out)


def _kv_kernel(x_ref, mod_ref, ng_ref, wk_ref, wvt_ref, k_ref, vt_ref, km_ref):
    d = x_ref.shape[2]
    n_heads = k_ref.shape[1]
    mod = mod_ref[0]
    h = _norm_modulate(x_ref[0], ng_ref[...], mod[:, :d], mod[:, d:]).astype(BF16)
    k = jnp.dot(h, wk_ref[...], preferred_element_type=F32)
    vt = lax.dot_general(wvt_ref[...], h, NT_DIMS, preferred_element_type=F32)
    for j in range(km_ref.shape[1]):
        km_ref[0, j] = jnp.mean(k[j * MOBA_BLOCK:(j + 1) * MOBA_BLOCK], axis=0, keepdims=True)
    pad_row = lax.broadcasted_iota(jnp.int32, (VT_ROWS - HEAD_DIM, vt.shape[1]), 0)
    ones_then_zeros = jnp.where(pad_row == 0, 1.0, 0.0).astype(BF16)
    for hd in range(n_heads):
        sl = slice(hd * HEAD_DIM, (hd + 1) * HEAD_DIM)
        k_ref[0, hd] = k[:, sl].astype(BF16)
        vt_ref[0, hd, :HEAD_DIM, :] = vt[sl, :].astype(BF16)
        vt_ref[0, hd, HEAD_DIM:, :] = ones_then_zeros


def _shared_kv(x, mod, norm_g, w_k, w_v_t, tile):
    bsz, seq, d = x.shape
    d_att = w_k.shape[1]
    n_heads = d_att // HEAD_DIM
    n_blk = seq // MOBA_BLOCK
    blk_per_tile = tile // MOBA_BLOCK
    assert tile % MOBA_BLOCK == 0 and seq % tile == 0
    return pl.pallas_call(
        _kv_kernel,
        grid=(bsz, seq // tile),
        in_specs=[
            pl.BlockSpec((1, tile, d), lambda b, i: (b, i, 0)),
            pl.BlockSpec((1, 1, 2 * d), lambda b, i: (b, 0, 0)),
            pl.BlockSpec((1, d), lambda b, i: (0, 0)),
            pl.BlockSpec((d, d_att), lambda b, i: (0, 0)),
            pl.BlockSpec((d_att, d), lambda b, i: (0, 0)),
        ],
        out_specs=[
            pl.BlockSpec((1, n_heads, tile, HEAD_DIM), lambda b, i: (b, 0, i, 0)),
            pl.BlockSpec((1, n_heads, VT_ROWS, tile), lambda b, i: (b, 0, 0, i)),
            pl.BlockSpec((1, blk_per_tile, 1, d_att), lambda b, i: (b, i, 0, 0)),
        ],
        out_shape=[jax.ShapeDtypeStruct((bsz, n_heads, seq, HEAD_DIM), BF16),
                   jax.ShapeDtypeStruct((bsz, n_heads, VT_ROWS, seq), BF16),
                   jax.ShapeDtypeStruct((bsz, n_blk, 1, d_att), F32)],
        compiler_params=pltpu.CompilerParams(
            dimension_semantics=("arbitrary", "arbitrary"),
            vmem_limit_bytes=VMEM_LIMIT_BYTES),
        name="shared_kv",
    )(x, mod, norm_g, w_k, w_v_t)


def _select_blocks(gate_t, n_past, sel_s):
    blk = lax.broadcasted_iota(jnp.int32, gate_t.shape, 0)
    g = jnp.where(blk < n_past, gate_t, NEG_INF)
    sel_s[...] = g
    rank = jnp.zeros(gate_t.shape, jnp.int32)
    for j in range(n_past):
        gj = jnp.broadcast_to(sel_s[j:j + 1, :], gate_t.shape)
        ahead = (gj > g) | ((gj == g) & (blk > j))
        rank = rank + ahead.astype(jnp.int32)
    sel_s[...] = jnp.where((rank < MOBA_TOPK) & (blk < n_past), 0.0, NEG_INF)


def _fold_rows(a, op):
    return op(a.reshape(a.shape[0] // SUBLANES, SUBLANES, a.shape[1]), axis=0)


MAX_HEADS_PER_STAGE = 8


def _heads_per_stage(n_past, slot_rows):
    per_stage = 1
    while (per_stage < MAX_HEADS_PER_STAGE
           and 2 * per_stage * (n_past + 1) * MOBA_BLOCK <= slot_rows):
        per_stage *= 2
    return per_stage


def _head_scores(hd, slot, lane, n_past, q_s, k_ref, km_ref, sel_s, sc_s, m_s):
    qh = q_s[hd]
    masked = n_past > MOBA_TOPK
    if masked:
        gate_t = lax.dot_general(km_ref[0, hd].astype(BF16), qh, NT_DIMS,
                                 preferred_element_type=F32)
        sel_s = sel_s.at[lane]
        _select_blocks(gate_t, n_past, sel_s)
    n_keys = (n_past + 1) * MOBA_BLOCK
    s_all = lax.dot_general(k_ref[0, hd, 0:n_keys, :], qh, NT_DIMS,
                            preferred_element_type=F32)
    m8 = None
    for j in range(n_past + 1):
        rows = slice(j * MOBA_BLOCK, (j + 1) * MOBA_BLOCK)
        s = s_all[rows]
        if j == n_past:
            key = lax.broadcasted_iota(jnp.int32, s.shape, 0)
            qry = lax.broadcasted_iota(jnp.int32, s.shape, 1)
            s = jnp.where(key <= qry, s, NEG_INF)
        elif masked:
            s = s + sel_s[j:j + 1, :]
        sc_s[slot, lane * n_keys + j * MOBA_BLOCK:lane * n_keys + (j + 1) * MOBA_BLOCK] = s
        smax = _fold_rows(s, jnp.max)
        m8 = smax if m8 is None else jnp.maximum(m8, smax)
    m_s[slot, lane] = m8


def _head_output(hd, slot, lane, n_past, vt_ref, sc_s, m_s, o_s):
    n_keys = (n_past + 1) * MOBA_BLOCK
    m = jnp.max(m_s[slot, lane], axis=0, keepdims=True)
    p = jnp.exp2(sc_s[slot, lane * n_keys:(lane + 1) * n_keys] - m).astype(BF16)
    o_t = jnp.dot(vt_ref[0, hd, :, 0:n_keys], p, preferred_element_type=F32)
    o_s[hd] = o_t[:HEAD_DIM] * (1.0 / o_t[HEAD_DIM:HEAD_DIM + 1])


def _moba_kernel(x_ref, mod_ref, ng_ref, win_ref, k_ref, vt_ref, km_ref, wout_ref, fg_ref,
                 o_ref, x_s, hb_s, q_s, g_s, o_s, sel_s, sc_s, m_s):
    d = x_ref.shape[2]
    n_layers = win_ref.shape[0]
    n_heads = k_ref.shape[1]
    n_blk = km_ref.shape[2]
    d_att = n_heads * HEAD_DIM
    qb = pl.program_id(1)
    scores = functools.partial(_head_scores, q_s=q_s, k_ref=k_ref, km_ref=km_ref, sel_s=sel_s,
                               sc_s=sc_s, m_s=m_s)
    output = functools.partial(_head_output, vt_ref=vt_ref, sc_s=sc_s, m_s=m_s, o_s=o_s)

    def project_q(lyr):
        mod = mod_ref[0, pl.ds(lyr, 1), :]
        hb = _norm_modulate(x_s[...], ng_ref[lyr], mod[:, :d], mod[:, d:2 * d]).astype(BF16)
        hb_s[...] = hb
        q = jnp.dot(hb, win_ref[lyr, :, :d_att], preferred_element_type=F32)
        q = (q * (HEAD_DIM ** -0.5 * LOG2E)).astype(BF16)
        for hd in range(n_heads):
            q_s[hd] = q[:, hd * HEAD_DIM:(hd + 1) * HEAD_DIM]

    def attend_and_project(n_past, lyr):
        per_stage = _heads_per_stage(n_past, sc_s.shape[1])
        n_groups = n_heads // per_stage
        assert sc_s.shape[0] >= min(n_groups, 2)

        def stage(fn, group):
            for lane in range(per_stage):
                fn(group * per_stage + lane, group & 1, lane, n_past)

        stage(scores, 0)
        g_s[...] = jnp.dot(hb_s[...], win_ref[lyr, :, d_att:], preferred_element_type=F32)

        def step(group, carry):
            stage(output, group - 1)
            stage(scores, group)
            return carry

        lax.fori_loop(1, n_groups, step, 0)
        stage(output, n_groups - 1)

        ys = []
        for hd in range(n_heads):
            gp = g_s[:, hd * HEAD_DIM:(hd + 1) * HEAD_DIM]
            ys.append((o_s[hd].T * (gp * _sigmoid(gp))).astype(BF16))
        early = (n_heads - 1) * HEAD_DIM // MXU_DIM * MXU_DIM
        proj = jnp.dot(jnp.concatenate(ys[:early // HEAD_DIM], axis=1), wout_ref[lyr, :early, :],
                       preferred_element_type=F32)
        proj = proj + jnp.dot(jnp.concatenate(ys[early // HEAD_DIM:], axis=1),
                              wout_ref[lyr, early:, :], preferred_element_type=F32)
        gate = mod_ref[0, pl.ds(lyr, 1), 2 * d:]
        x_s[...] = x_s[...] + gate * proj

    x_s[...] = x_ref[0]

    def one_layer(lyr, carry):
        project_q(lyr)
        for n_past in range(n_blk):
            pl.when(qb == n_past)(functools.partial(attend_and_project, n_past, lyr))
        return carry

    lax.fori_loop(0, n_layers, one_layer, 0)
    out = x_s[...]
    ms = jnp.mean(out * out, axis=-1, keepdims=True)
    o_ref[0] = out * lax.rsqrt(ms + EPS) * fg_ref[...]


def _moba_stack(x, mod, norm_g, w_in, k, v_t, k_mean, w_out, final_g):
    bsz, seq, d = x.shape
    n_layers = w_in.shape[0]
    n_heads = k.shape[1]
    d_att = n_heads * HEAD_DIM
    n_blk = seq // MOBA_BLOCK
    full = lambda shape: pl.BlockSpec(shape, lambda b, i: (0,) * len(shape))
    per_batch = lambda shape, **kw: pl.BlockSpec(
        shape, lambda b, i: (b,) + (0,) * (len(shape) - 1), **kw)
    return pl.pallas_call(
        _moba_kernel,
        grid=(bsz, n_blk),
        in_specs=[
            pl.BlockSpec((1, MOBA_BLOCK, d), lambda b, i: (b, i, 0)),
            per_batch((1, n_layers, 3 * d)),
            full((n_layers, 1, d)),
            full((n_layers, d, 2 * d_att)),
            per_batch((1, n_heads, seq, HEAD_DIM)),
            per_batch((1, n_heads, VT_ROWS, seq), pipeline_mode=pl.Buffered(1)),
            per_batch((1, n_heads, n_blk, HEAD_DIM)),
            full((n_layers, d_att, d)),
            full((1, d)),
        ],
        out_specs=pl.BlockSpec((1, MOBA_BLOCK, d), lambda b, i: (b, i, 0)),
        out_shape=jax.ShapeDtypeStruct(x.shape, F32),
        scratch_shapes=[
            pltpu.VMEM((MOBA_BLOCK, d), F32),
            pltpu.VMEM((MOBA_BLOCK, d), BF16),
            pltpu.VMEM((n_heads, MOBA_BLOCK, HEAD_DIM), BF16),
            pltpu.VMEM((MOBA_BLOCK, d_att), F32),
            pltpu.VMEM((n_heads, HEAD_DIM, MOBA_BLOCK), F32),
            pltpu.VMEM((MAX_HEADS_PER_STAGE, n_blk, MOBA_BLOCK), F32),
            pltpu.VMEM((1, n_heads * seq, MOBA_BLOCK), F32),
            pltpu.VMEM((2, MAX_HEADS_PER_STAGE, SUBLANES, MOBA_BLOCK), F32),
        ],
        compiler_params=pltpu.CompilerParams(
            dimension_semantics=("arbitrary", "arbitrary"),
            vmem_limit_bytes=VMEM_LIMIT_BYTES),
        name="moba_stack",
    )(x, mod, norm_g, w_in, k, v_t, k_mean, w_out, final_g)


def kernel(x, c, mod_w, mod_b, norm_g, rg_w_in, rg_conv_w, rg_conv_b, rg_w_a, rg_b_a, rg_w_x,
           rg_b_x, rg_lambda, rg_w_out, kv_norm_g, kv_mod_w, kv_mod_b, w_kv, att_w_in,
           att_w_out, final_norm_g):
    bsz, seq, d = x.shape
    depth = mod_w.shape[0]
    n_a = rg_w_in.shape[0]
    n_b = att_w_in.shape[0]
    assert depth == n_a + n_b and seq % MOBA_BLOCK == 0
    d_att = w_kv.shape[1] // 2
    n_heads = d_att // HEAD_DIM
    n_blk = seq // MOBA_BLOCK

    mod = _modulation(c, mod_w, mod_b, tn=3 * d)
    kv_mod = _modulation(c, kv_mod_w[None], kv_mod_b[None], tn=d)

    row = lambda p: p.reshape(1, -1)
    rows = lambda p: p[:, None, :]
    w_ax = jnp.concatenate([rg_w_a, rg_w_x], axis=-1).astype(BF16)
    x = _rglru_stack(
        x, mod[:n_a], rows(norm_g[:n_a]), rg_w_in.astype(BF16), rg_conv_w, rows(rg_conv_b), w_ax,
        rows(rg_b_a), rows(rg_b_x), rows(rg_lambda), rg_w_out.astype(BF16), t_len=128)

    w_kv_b = w_kv.astype(BF16)
    k, v_t, k_mean = _shared_kv(x, kv_mod[0][:, None, :], row(kv_norm_g),
                                w_kv_b[:, :d_att], w_kv_b[:, d_att:].T, tile=4 * MOBA_BLOCK)
    k_mean = k_mean.reshape(bsz, n_blk, n_heads, HEAD_DIM).transpose(0, 2, 1, 3)

    return _moba_stack(
        x, mod[n_a:].transpose(1, 0, 2), norm_g[n_a:, None, :], att_w_in.astype(BF16), k, v_t,
        k_mean, att_w_out.astype(BF16), row(final_norm_g))
```

```python
import functools

import jax
import jax.numpy as jnp
from jax import lax
from jax.experimental import pallas as pl
from jax.experimental.pallas import tpu as pltpu

EPS = 1e-6
RG_C = 8.0
HEAD_DIM = 128
MOBA_BLOCK = 256
MOBA_TOPK = 3
NEG_INF = -1e30
LOG2E = 1.4426950408889634
SUBLANES = 8
BF16_SUBLANES = 16
MXU_DIM = 256
VT_ROWS = HEAD_DIM + BF16_SUBLANES
VMEM_LIMIT_BYTES = 56 * 1024 * 1024
RGLRU_TILE_STEPS = 128
KV_TILE_ROWS = 4 * MOBA_BLOCK

F32 = jnp.float32
BF16 = jnp.bfloat16
NT_DIMS = (((1,), (1,)), ((), ()))


def _sigmoid(z):
    return 1.0 / (1.0 + jnp.exp2(z * (-LOG2E)))


def _sqrt_nonneg(v):
    return jnp.where(v > 0.0, v * lax.rsqrt(v), 0.0)


def _norm_modulate(x, norm_g, shift, scale):
    ms = jnp.mean(x * x, axis=-1, keepdims=True)
    return x * lax.rsqrt(ms + EPS) * (norm_g * (1.0 + scale)) + shift


def _mod_kernel(c_ref, w_ref, b_ref, o_ref):
    c = c_ref[...]
    cs = (c * _sigmoid(c)).astype(BF16)
    w = w_ref[0].astype(BF16)
    o_ref[0] = jnp.dot(cs, w, preferred_element_type=F32) + b_ref[0]


def _modulation(c, w, b, tn):
    n_layers, d, n = w.shape
    bsz = c.shape[0]
    return pl.pallas_call(
        _mod_kernel,
        grid=(n_layers, n // tn),
        in_specs=[
            pl.BlockSpec((bsz, d), lambda l, j: (0, 0)),
            pl.BlockSpec((1, d, tn), lambda l, j: (l, 0, j)),
            pl.BlockSpec((1, 1, tn), lambda l, j: (l, 0, j)),
        ],
        out_specs=pl.BlockSpec((1, bsz, tn), lambda l, j: (l, 0, j)),
        out_shape=jax.ShapeDtypeStruct((n_layers, bsz, n), F32),
        compiler_params=pltpu.CompilerParams(
            dimension_semantics=("arbitrary", "arbitrary"),
            vmem_limit_bytes=VMEM_LIMIT_BYTES),
        name="adaln_mod",
    )(c, w, b.reshape(n_layers, 1, n))


def _tile_copies(hbm_ref, buf, sem, tile, slot, to_hbm):
    t_len, bsz, _ = buf.shape[1:]
    copies = []
    for b in range(bsz):
        hbm = hbm_ref.at[b, pl.ds(tile * t_len, t_len), :]
        vmem = buf.at[slot, :, b, :]
        src, dst = (vmem, hbm) if to_hbm else (hbm, vmem)
        copies.append(pltpu.make_async_copy(src, dst, sem.at[slot]))
    return copies


def _rglru_kernel(x_hbm, mod_ref, ng_ref, win_ref, cw_ref, cb_ref, wax_ref, ba_ref,
                  bx_ref, lam_ref, wout_ref, o_hbm,
                  xbuf, obuf, in_sem, out_sem, hs_s, y_s, utail, hstate):
    _, t_len, bsz, d = xbuf.shape
    n_layers, n_heads, rb, _ = wax_ref.shape
    conv_w = cw_ref.shape[1]
    i = pl.program_id(0)
    n_tiles = x_hbm.shape[1] // t_len
    slot = i & 1

    @pl.when(i == 0)
    def _():
        for c in _tile_copies(x_hbm, xbuf, in_sem, 0, 0, to_hbm=False):
            c.start()
        utail[...] = jnp.zeros_like(utail)
        hstate[...] = jnp.zeros_like(hstate)

    @pl.when(i + 1 < n_tiles)
    def _():
        for c in _tile_copies(x_hbm, xbuf, in_sem, i + 1, 1 - slot, to_hbm=False):
            c.start()

    @pl.when(i >= 2)
    def _():
        for c in _tile_copies(o_hbm, obuf, out_sem, i - 2, slot, to_hbm=True):
            c.wait()

    for c in _tile_copies(x_hbm, xbuf, in_sem, i, slot, to_hbm=False):
        c.wait()

    rows = t_len * bsz

    def layer(lyr, x3):
        mod = mod_ref[lyr]
        shift, scale, gate = mod[:, :d], mod[:, d:2 * d], mod[:, 2 * d:]
        lam = lam_ref[lyr]
        softplus_neg_lam = jnp.maximum(-lam, 0.0) + jnp.log1p(jnp.exp(-jnp.abs(lam)))
        log2_a_per_r = (-RG_C * LOG2E) * softplus_neg_lam

        ms = jnp.mean(x3 * x3, axis=-1, keepdims=True)
        h3 = x3 * lax.rsqrt(ms + EPS) * (ng_ref[lyr] * (1.0 + scale)) + shift
        hb = h3.reshape(rows, d).astype(BF16)
        u3 = jnp.dot(hb, win_ref[lyr, :, :d], preferred_element_type=F32).reshape(t_len, bsz, d)

        upad = jnp.concatenate([utail[lyr], u3], axis=0)
        utail[lyr] = u3[t_len - (conv_w - 1):]
        uc3 = cb_ref[lyr] + cw_ref[lyr, conv_w - 1:conv_w, :] * u3
        for k in range(conv_w - 1):
            uc3 = uc3 + cw_ref[lyr, k:k + 1, :] * upad[k:k + t_len]
        uc = uc3.reshape(rows, d)

        for hh in range(n_heads):
            sl = slice(hh * rb, (hh + 1) * rb)
            uch = uc[:, sl]
            z = jnp.dot(uch.astype(BF16), wax_ref[lyr, hh], preferred_element_type=F32)
            r = _sigmoid(z[:, :rb] + ba_ref[lyr, :, sl])
            gi = _sigmoid(z[:, rb:] + bx_ref[lyr, :, sl])
            a = jnp.exp2(r * log2_a_per_r[:, sl])
            b_in = _sqrt_nonneg(1.0 - a * a) * (gi * uch)
            a3 = a.reshape(t_len, bsz, rb)
            b3 = b_in.reshape(t_len, bsz, rb)
            h_run = hstate[lyr, :, sl]
            for t in range(t_len):
                h_run = a3[t] * h_run + b3[t]
                hs_s[t, :, sl] = h_run
            hstate[lyr, :, sl] = h_run
            gpath = jnp.dot(hb, win_ref[lyr, :, d + hh * rb:d + (hh + 1) * rb],
                            preferred_element_type=F32)
            y = hs_s[:, :, sl].reshape(rows, rb) * (gpath * _sigmoid(gpath))
            y_s[:, sl] = y.astype(BF16)

        proj = jnp.dot(y_s[...], wout_ref[lyr], preferred_element_type=F32)
        return x3 + gate * proj.reshape(t_len, bsz, d)

    x3 = xbuf[slot]
    for lyr in range(n_layers):
        x3 = layer(lyr, x3)
    obuf[slot] = x3

    for c in _tile_copies(o_hbm, obuf, out_sem, i, slot, to_hbm=True):
        c.start()

    @pl.when(i == n_tiles - 1)
    def _():
        if n_tiles >= 2:
            for c in _tile_copies(o_hbm, obuf, out_sem, i - 1, 1 - slot, to_hbm=True):
                c.wait()
        for c in _tile_copies(o_hbm, obuf, out_sem, i, slot, to_hbm=True):
            c.wait()


def _rglru_stack(x, mod, norm_g, w_in, conv_w, conv_b, w_ax, b_a, b_x, lam, w_out, t_len):
    bsz, seq, d = x.shape
    n_conv = conv_w.shape[1]
    assert bsz == SUBLANES and seq % t_len == 0 and t_len >= n_conv - 1
    n_layers = w_in.shape[0]
    full = lambda a: pl.BlockSpec(a.shape, lambda i: (0,) * a.ndim)
    return pl.pallas_call(
        _rglru_kernel,
        grid=(seq // t_len,),
        in_specs=[pl.BlockSpec(memory_space=pl.ANY)] + [
            full(a) for a in (mod, norm_g, w_in, conv_w, conv_b, w_ax, b_a, b_x, lam, w_out)],
        out_specs=pl.BlockSpec(memory_space=pl.ANY),
        out_shape=jax.ShapeDtypeStruct(x.shape, F32),
        scratch_shapes=[
            pltpu.VMEM((2, t_len, bsz, d), F32),
            pltpu.VMEM((2, t_len, bsz, d), F32),
            pltpu.SemaphoreType.DMA((2,)),
            pltpu.SemaphoreType.DMA((2,)),
            pltpu.VMEM((t_len, bsz, d), F32),
            pltpu.VMEM((t_len * bsz, d), BF16),
            pltpu.VMEM((n_layers, n_conv - 1, bsz, d), F32),
            pltpu.VMEM((n_layers, bsz, d), F32),
        ],
        compiler_params=pltpu.CompilerParams(
            dimension_semantics=("arbitrary",),
            vmem_limit_bytes=VMEM_LIMIT_BYTES),
        name="rglru_stack",
    )(x, mod, norm_g, w_in, conv_w, conv_b, w_ax, b_a, b_x, lam, w_out)


def _kv_kernel(x_ref, mod_ref, ng_ref, wk_ref, wvt_ref, k_ref, vt_ref, km_ref):
    d = x_ref.shape[2]
    n_heads = k_ref.shape[1]
    mod = mod_ref[0]
    h = _norm_modulate(x_ref[0], ng_ref[...], mod[:, :d], mod[:, d:]).astype(BF16)
    k = jnp.dot(h, wk_ref[...], preferred_element_type=F32)
    vt = lax.dot_general(wvt_ref[...], h, NT_DIMS, preferred_element_type=F32)
    for j in range(km_ref.shape[1]):
        km_ref[0, j] = jnp.mean(k[j * MOBA_BLOCK:(j + 1) * MOBA_BLOCK], axis=0, keepdims=True)
    pad_row = lax.broadcasted_iota(jnp.int32, (VT_ROWS - HEAD_DIM, vt.shape[1]), 0)
    ones_then_zeros = jnp.where(pad_row == 0, 1.0, 0.0).astype(BF16)
    for hd in range(n_heads):
        sl = slice(hd * HEAD_DIM, (hd + 1) * HEAD_DIM)
        k_ref[0, hd] = k[:, sl].astype(BF16)
        vt_ref[0, hd, :HEAD_DIM, :] = vt[sl, :].astype(BF16)
        vt_ref[0, hd, HEAD_DIM:, :] = ones_then_zeros


def _shared_kv(x, mod, norm_g, w_k, w_v_t, tile):
    bsz, seq, d = x.shape
    d_att = w_k.shape[1]
    n_heads = d_att // HEAD_DIM
    n_blk = seq // MOBA_BLOCK
    blk_per_tile = tile // MOBA_BLOCK
    assert tile % MOBA_BLOCK == 0 and seq % tile == 0
    return pl.pallas_call(
        _kv_kernel,
        grid=(bsz, seq // tile),
        in_specs=[
            pl.BlockSpec((1, tile, d), lambda b, i: (b, i, 0)),
            pl.BlockSpec((1, 1, 2 * d), lambda b, i: (b, 0, 0)),
            pl.BlockSpec((1, d), lambda b, i: (0, 0)),
            pl.BlockSpec((d, d_att), lambda b, i: (0, 0)),
            pl.BlockSpec((d_att, d), lambda b, i: (0, 0)),
        ],
        out_specs=[
            pl.BlockSpec((1, n_heads, tile, HEAD_DIM), lambda b, i: (b, 0, i, 0)),
            pl.BlockSpec((1, n_heads, VT_ROWS, tile), lambda b, i: (b, 0, 0, i)),
            pl.BlockSpec((1, blk_per_tile, 1, d_att), lambda b, i: (b, i, 0, 0)),
        ],
        out_shape=[jax.ShapeDtypeStruct((bsz, n_heads, seq, HEAD_DIM), BF16),
                   jax.ShapeDtypeStruct((bsz, n_heads, VT_ROWS, seq), BF16),
                   jax.ShapeDtypeStruct((bsz, n_blk, 1, d_att), F32)],
        compiler_params=pltpu.CompilerParams(
            dimension_semantics=("arbitrary", "arbitrary"),
            vmem_limit_bytes=VMEM_LIMIT_BYTES),
        name="shared_kv",
    )(x, mod, norm_g, w_k, w_v_t)


def _select_blocks(gate_t, n_past, sel_s):
    blk = lax.broadcasted_iota(jnp.int32, gate_t.shape, 0)
    g = jnp.where(blk < n_past, gate_t, NEG_INF)
    sel_s[...] = g
    rank = jnp.zeros(gate_t.shape, jnp.int32)
    for j in range(n_past):
        gj = jnp.broadcast_to(sel_s[j:j + 1, :], gate_t.shape)
        ahead = (gj > g) | ((gj == g) & (blk > j))
        rank = rank + ahead.astype(jnp.int32)
    sel_s[...] = jnp.where((rank < MOBA_TOPK) & (blk < n_past), 0.0, NEG_INF)


def _fold_rows(a, op):
    return op(a.reshape(a.shape[0] // SUBLANES, SUBLANES, a.shape[1]), axis=0)


MAX_HEADS_PER_STAGE = 8


def _heads_per_stage(n_past, slot_rows):
    per_stage = 1
    while (per_stage < MAX_HEADS_PER_STAGE
           and 2 * per_stage * (n_past + 1) * MOBA_BLOCK <= slot_rows):
        per_stage *= 2
    return per_stage


def _head_scores(hd, slot, lane, n_past, q_s, k_ref, km_ref, sel_s, sc_s, m_s):
    qh = q_s[hd]
    masked = n_past > MOBA_TOPK
    if masked:
        gate_t = lax.dot_general(km_ref[0, hd].astype(BF16), qh, NT_DIMS,
                                 preferred_element_type=F32)
        sel_s = sel_s.at[lane]
        _select_blocks(gate_t, n_past, sel_s)
    n_keys = (n_past + 1) * MOBA_BLOCK
    s_all = lax.dot_general(k_ref[0, hd, 0:n_keys, :], qh, NT_DIMS,
                            preferred_element_type=F32)
    m8 = None
    for j in range(n_past + 1):
        rows = slice(j * MOBA_BLOCK, (j + 1) * MOBA_BLOCK)
        s = s_all[rows]
        if j == n_past:
            key = lax.broadcasted_iota(jnp.int32, s.shape, 0)
            qry = lax.broadcasted_iota(jnp.int32, s.shape, 1)
            s = jnp.where(key <= qry, s, NEG_INF)
        elif masked:
            s = s + sel_s[j:j + 1, :]
        sc_s[slot, lane * n_keys + j * MOBA_BLOCK:lane * n_keys + (j + 1) * MOBA_BLOCK] = s
        smax = _fold_rows(s, jnp.max)
        m8 = smax if m8 is None else jnp.maximum(m8, smax)
    m_s[slot, lane] = m8


def _head_output(hd, slot, lane, n_past, vt_ref, sc_s, m_s, o_s):
    n_keys = (n_past + 1) * MOBA_BLOCK
    m = jnp.max(m_s[slot, lane], axis=0, keepdims=True)
    p = jnp.exp2(sc_s[slot, lane * n_keys:(lane + 1) * n_keys] - m).astype(BF16)
    o_t = jnp.dot(vt_ref[0, hd, :, 0:n_keys], p, preferred_element_type=F32)
    o_s[hd] = o_t[:HEAD_DIM] * (1.0 / o_t[HEAD_DIM:HEAD_DIM + 1])


def _moba_kernel(x_ref, mod_ref, ng_ref, win_ref, k_ref, vt_ref, km_ref, wout_ref, fg_ref,
                 o_ref, x_s, hb_s, q_s, g_s, o_s, sel_s, sc_s, m_s):
    d = x_ref.shape[2]
    n_layers = win_ref.shape[0]
    n_heads = k_ref.shape[1]
    n_blk = km_ref.shape[2]
    d_att = n_heads * HEAD_DIM
    qb = pl.program_id(1)
    scores = functools.partial(_head_scores, q_s=q_s, k_ref=k_ref, km_ref=km_ref, sel_s=sel_s,
                               sc_s=sc_s, m_s=m_s)
    output = functools.partial(_head_output, vt_ref=vt_ref, sc_s=sc_s, m_s=m_s, o_s=o_s)

    def project_q(lyr):
        mod = mod_ref[0, pl.ds(lyr, 1), :]
        hb = _norm_modulate(x_s[...], ng_ref[lyr], mod[:, :d], mod[:, d:2 * d]).astype(BF16)
        hb_s[...] = hb
        q = jnp.dot(hb, win_ref[lyr, :, :d_att], preferred_element_type=F32)
        q = (q * (HEAD_DIM ** -0.5 * LOG2E)).astype(BF16)
        for hd in range(n_heads):
            q_s[hd] = q[:, hd * HEAD_DIM:(hd + 1) * HEAD_DIM]

    def attend_and_project(n_past, lyr):
        per_stage = _heads_per_stage(n_past, sc_s.shape[1])
        n_groups = n_heads // per_stage
        assert sc_s.shape[0] >= min(n_groups, 2)

        def stage(fn, group):
            for lane in range(per_stage):
                fn(group * per_stage + lane, group & 1, lane, n_past)

        stage(scores, 0)
        g_s[...] = jnp.dot(hb_s[...], win_ref[lyr, :, d_att:], preferred_element_type=F32)

        def step(group, carry):
            stage(output, group - 1)
            stage(scores, group)
            return carry

        lax.fori_loop(1, n_groups, step, 0)
        stage(output, n_groups - 1)

        ys = []
        for hd in range(n_heads):
            gp = g_s[:, hd * HEAD_DIM:(hd + 1) * HEAD_DIM]
            ys.append((o_s[hd].T * (gp * _sigmoid(gp))).astype(BF16))
        early = (n_heads - 1) * HEAD_DIM // MXU_DIM * MXU_DIM
        proj = jnp.dot(jnp.concatenate(ys[:early // HEAD_DIM], axis=1), wout_ref[lyr, :early, :],
                       preferred_element_type=F32)
        proj = proj + jnp.dot(jnp.concatenate(ys[early // HEAD_DIM:], axis=1),
                              wout_ref[lyr, early:, :], preferred_element_type=F32)
        gate = mod_ref[0, pl.ds(lyr, 1), 2 * d:]
        x_s[...] = x_s[...] + gate * proj

    x_s[...] = x_ref[0]

    def one_layer(lyr, carry):
        project_q(lyr)
        for n_past in range(n_blk):
            pl.when(qb == n_past)(functools.partial(attend_and_project, n_past, lyr))
        return carry

    lax.fori_loop(0, n_layers, one_layer, 0)
    out = x_s[...]
    ms = jnp.mean(out * out, axis=-1, keepdims=True)
    o_ref[0] = out * lax.rsqrt(ms + EPS) * fg_ref[...]


def _moba_stack(x, mod, norm_g, w_in, k, v_t, k_mean, w_out, final_g):
    bsz, seq, d = x.shape
    n_layers = w_in.shape[0]
    n_heads = k.shape[1]
    d_att = n_heads * HEAD_DIM
    n_blk = seq // MOBA_BLOCK
    full = lambda shape: pl.BlockSpec(shape, lambda b, i: (0,) * len(shape))
    per_batch = lambda shape, **kw: pl.BlockSpec(
        shape, lambda b, i: (b,) + (0,) * (len(shape) - 1), **kw)
    return pl.pallas_call(
        _moba_kernel,
        grid=(bsz, n_blk),
        in_specs=[
            pl.BlockSpec((1, MOBA_BLOCK, d), lambda b, i: (b, i, 0)),
            per_batch((1, n_layers, 3 * d)),
            full((n_layers, 1, d)),
            full((n_layers, d, 2 * d_att)),
            per_batch((1, n_heads, seq, HEAD_DIM)),
            per_batch((1, n_heads, VT_ROWS, seq), pipeline_mode=pl.Buffered(1)),
            per_batch((1, n_heads, n_blk, HEAD_DIM)),
            full((n_layers, d_att, d)),
            full((1, d)),
        ],
        out_specs=pl.BlockSpec((1, MOBA_BLOCK, d), lambda b, i: (b, i, 0)),
        out_shape=jax.ShapeDtypeStruct(x.shape, F32),
        scratch_shapes=[
            pltpu.VMEM((MOBA_BLOCK, d), F32),
            pltpu.VMEM((MOBA_BLOCK, d), BF16),
            pltpu.VMEM((n_heads, MOBA_BLOCK, HEAD_DIM), BF16),
            pltpu.VMEM((MOBA_BLOCK, d_att), F32),
            pltpu.VMEM((n_heads, HEAD_DIM, MOBA_BLOCK), F32),
            pltpu.VMEM((MAX_HEADS_PER_STAGE, n_blk, MOBA_BLOCK), F32),
            pltpu.VMEM((1, n_heads * seq, MOBA_BLOCK), F32),
            pltpu.VMEM((2, MAX_HEADS_PER_STAGE, SUBLANES, MOBA_BLOCK), F32),
        ],
        compiler_params=pltpu.CompilerParams(
            dimension_semantics=("arbitrary", "arbitrary"),
            vmem_limit_bytes=VMEM_LIMIT_BYTES),
        name="moba_stack",
    )(x, mod, norm_g, w_in, k, v_t, k_mean, w_out, final_g)


def kernel(x, c, mod_w, mod_b, norm_g, rg_w_in, rg_conv_w, rg_conv_b, rg_w_a, rg_b_a, rg_w_x,
           rg_b_x, rg_lambda, rg_w_out, kv_norm_g, kv_mod_w, kv_mod_b, w_kv, att_w_in,
           att_w_out, final_norm_g):
    bsz, seq, d = x.shape
    depth = mod_w.shape[0]
    n_a = rg_w_in.shape[0]
    n_b = att_w_in.shape[0]
    assert depth == n_a + n_b and seq % MOBA_BLOCK == 0
    d_att = w_kv.shape[1] // 2
    n_heads = d_att // HEAD_DIM
    n_blk = seq // MOBA_BLOCK

    mod = _modulation(c, mod_w, mod_b, tn=3 * d // 2)
    kv_mod = _modulation(c, kv_mod_w[None], kv_mod_b[None], tn=d)

    row = lambda p: p.reshape(1, -1)
    rows = lambda p: p[:, None, :]
    w_ax = jnp.concatenate([rg_w_a, rg_w_x], axis=-1).astype(BF16)
    x = _rglru_stack(
        x, mod[:n_a], rows(norm_g[:n_a]), rg_w_in.astype(BF16), rg_conv_w, rows(rg_conv_b), w_ax,
        rows(rg_b_a), rows(rg_b_x), rows(rg_lambda), rg_w_out.astype(BF16),
        t_len=RGLRU_TILE_STEPS)

    w_kv_b = w_kv.astype(BF16)
    k, v_t, k_mean = _shared_kv(x, kv_mod[0][:, None, :], row(kv_norm_g),
                                w_kv_b[:, :d_att], w_kv_b[:, d_att:].T, tile=KV_TILE_ROWS)
    k_mean = k_mean.reshape(bsz, n_blk, n_heads, HEAD_DIM).transpose(0, 2, 1, 3)

    return _moba_stack(
        x, mod[n_a:].transpose(1, 0, 2), norm_g[n_a:, None, :], att_w_in.astype(BF16), k, v_t,
        k_mean, att_w_out.astype(BF16), row(final_norm_g))
```

```python
import functools

import jax
import jax.numpy as jnp
from jax import lax
from jax.experimental import pallas as pl
from jax.experimental.pallas import tpu as pltpu

EPS = 1e-6
RG_C = 8.0
HEAD_DIM = 128
MOBA_BLOCK = 256
MOBA_TOPK = 3
NEG_INF = -1e30
LOG2E = 1.4426950408889634
SUBLANES = 8
BF16_SUBLANES = 16
MXU_DIM = 256
VT_ROWS = HEAD_DIM + BF16_SUBLANES
VMEM_LIMIT_BYTES = 56 * 1024 * 1024
RGLRU_TILE_STEPS = 128
KV_TILE_ROWS = 8 * MOBA_BLOCK

F32 = jnp.float32
BF16 = jnp.bfloat16
NT_DIMS = (((1,), (1,)), ((), ()))


def _sigmoid(z):
    return 1.0 / (1.0 + jnp.exp2(z * (-LOG2E)))


def _sqrt_nonneg(v):
    return jnp.where(v > 0.0, v * lax.rsqrt(v), 0.0)


def _norm_modulate(x, norm_g, shift, scale):
    ms = jnp.mean(x * x, axis=-1, keepdims=True)
    return x * lax.rsqrt(ms + EPS) * (norm_g * (1.0 + scale)) + shift


def _mod_kernel(c_ref, w_ref, b_ref, o_ref):
    c = c_ref[...]
    cs = (c * _sigmoid(c)).astype(BF16)
    w = w_ref[0].astype(BF16)
    o_ref[0] = jnp.dot(cs, w, preferred_element_type=F32) + b_ref[0]


def _modulation(c, w, b, tn):
    n_layers, d, n = w.shape
    bsz = c.shape[0]
    return pl.pallas_call(
        _mod_kernel,
        grid=(n_layers, n // tn),
        in_specs=[
            pl.BlockSpec((bsz, d), lambda l, j: (0, 0)),
            pl.BlockSpec((1, d, tn), lambda l, j: (l, 0, j)),
            pl.BlockSpec((1, 1, tn), lambda l, j: (l, 0, j)),
        ],
        out_specs=pl.BlockSpec((1, bsz, tn), lambda l, j: (l, 0, j)),
        out_shape=jax.ShapeDtypeStruct((n_layers, bsz, n), F32),
        compiler_params=pltpu.CompilerParams(
            dimension_semantics=("arbitrary", "arbitrary"),
            vmem_limit_bytes=VMEM_LIMIT_BYTES),
        name="adaln_mod",
    )(c, w, b.reshape(n_layers, 1, n))


def _tile_copies(hbm_ref, buf, sem, tile, slot, to_hbm):
    t_len, bsz, _ = buf.shape[1:]
    copies = []
    for b in range(bsz):
        hbm = hbm_ref.at[b, pl.ds(tile * t_len, t_len), :]
        vmem = buf.at[slot, :, b, :]
        src, dst = (vmem, hbm) if to_hbm else (hbm, vmem)
        copies.append(pltpu.make_async_copy(src, dst, sem.at[slot]))
    return copies


def _rglru_kernel(x_hbm, mod_ref, ng_ref, win_ref, cw_ref, cb_ref, wax_ref, ba_ref,
                  bx_ref, lam_ref, wout_ref, o_hbm,
                  xbuf, obuf, in_sem, out_sem, hs_s, y_s, utail, hstate):
    _, t_len, bsz, d = xbuf.shape
    n_layers, n_heads, rb, _ = wax_ref.shape
    conv_w = cw_ref.shape[1]
    i = pl.program_id(0)
    n_tiles = x_hbm.shape[1] // t_len
    slot = i & 1

    @pl.when(i == 0)
    def _():
        for c in _tile_copies(x_hbm, xbuf, in_sem, 0, 0, to_hbm=False):
            c.start()
        utail[...] = jnp.zeros_like(utail)
        hstate[...] = jnp.zeros_like(hstate)

    @pl.when(i + 1 < n_tiles)
    def _():
        for c in _tile_copies(x_hbm, xbuf, in_sem, i + 1, 1 - slot, to_hbm=False):
            c.start()

    @pl.when(i >= 2)
    def _():
        for c in _tile_copies(o_hbm, obuf, out_sem, i - 2, slot, to_hbm=True):
            c.wait()

    for c in _tile_copies(x_hbm, xbuf, in_sem, i, slot, to_hbm=False):
        c.wait()

    rows = t_len * bsz

    def layer(lyr, x3):
        mod = mod_ref[lyr]
        shift, scale, gate = mod[:, :d], mod[:, d:2 * d], mod[:, 2 * d:]
        lam = lam_ref[lyr]
        softplus_neg_lam = jnp.maximum(-lam, 0.0) + jnp.log1p(jnp.exp(-jnp.abs(lam)))
        log2_a_per_r = (-RG_C * LOG2E) * softplus_neg_lam

        ms = jnp.mean(x3 * x3, axis=-1, keepdims=True)
        h3 = x3 * lax.rsqrt(ms + EPS) * (ng_ref[lyr] * (1.0 + scale)) + shift
        hb = h3.reshape(rows, d).astype(BF16)
        u3 = jnp.dot(hb, win_ref[lyr, :, :d], preferred_element_type=F32).reshape(t_len, bsz, d)

        upad = jnp.concatenate([utail[lyr], u3], axis=0)
        utail[lyr] = u3[t_len - (conv_w - 1):]
        uc3 = cb_ref[lyr] + cw_ref[lyr, conv_w - 1:conv_w, :] * u3
        for k in range(conv_w - 1):
            uc3 = uc3 + cw_ref[lyr, k:k + 1, :] * upad[k:k + t_len]
        uc = uc3.reshape(rows, d)

        for hh in range(n_heads):
            sl = slice(hh * rb, (hh + 1) * rb)
            uch = uc[:, sl]
            z = jnp.dot(uch.astype(BF16), wax_ref[lyr, hh], preferred_element_type=F32)
            r = _sigmoid(z[:, :rb] + ba_ref[lyr, :, sl])
            gi = _sigmoid(z[:, rb:] + bx_ref[lyr, :, sl])
            a = jnp.exp2(r * log2_a_per_r[:, sl])
            b_in = _sqrt_nonneg(1.0 - a * a) * (gi * uch)
            a3 = a.reshape(t_len, bsz, rb)
            b3 = b_in.reshape(t_len, bsz, rb)
            h_run = hstate[lyr, :, sl]
            for t in range(t_len):
                h_run = a3[t] * h_run + b3[t]
                hs_s[t, :, sl] = h_run
            hstate[lyr, :, sl] = h_run
            gpath = jnp.dot(hb, win_ref[lyr, :, d + hh * rb:d + (hh + 1) * rb],
                            preferred_element_type=F32)
            y = hs_s[:, :, sl].reshape(rows, rb) * (gpath * _sigmoid(gpath))
            y_s[:, sl] = y.astype(BF16)

        proj = jnp.dot(y_s[...], wout_ref[lyr], preferred_element_type=F32)
        return x3 + gate * proj.reshape(t_len, bsz, d)

    x3 = xbuf[slot]
    for lyr in range(n_layers):
        x3 = layer(lyr, x3)
    obuf[slot] = x3

    for c in _tile_copies(o_hbm, obuf, out_sem, i, slot, to_hbm=True):
        c.start()

    @pl.when(i == n_tiles - 1)
    def _():
        if n_tiles >= 2:
            for c in _tile_copies(o_hbm, obuf, out_sem, i - 1, 1 - slot, to_hbm=True):
                c.wait()
        for c in _tile_copies(o_hbm, obuf, out_sem, i, slot, to_hbm=True):
            c.wait()


def _rglru_stack(x, mod, norm_g, w_in, conv_w, conv_b, w_ax, b_a, b_x, lam, w_out, t_len):
    bsz, seq, d = x.shape
    n_conv = conv_w.shape[1]
    assert bsz == SUBLANES and seq % t_len == 0 and t_len >= n_conv - 1
    n_layers = w_in.shape[0]
    full = lambda a: pl.BlockSpec(a.shape, lambda i: (0,) * a.ndim)
    return pl.pallas_call(
        _rglru_kernel,
        grid=(seq // t_len,),
        in_specs=[pl.BlockSpec(memory_space=pl.ANY)] + [
            full(a) for a in (mod, norm_g, w_in, conv_w, conv_b, w_ax, b_a, b_x, lam, w_out)],
        out_specs=pl.BlockSpec(memory_space=pl.ANY),
        out_shape=jax.ShapeDtypeStruct(x.shape, F32),
        scratch_shapes=[
            pltpu.VMEM((2, t_len, bsz, d), F32),
            pltpu.VMEM((2, t_len, bsz, d), F32),
            pltpu.SemaphoreType.DMA((2,)),
            pltpu.SemaphoreType.DMA((2,)),
            pltpu.VMEM((t_len, bsz, d), F32),
            pltpu.VMEM((t_len * bsz, d), BF16),
            pltpu.VMEM((n_layers, n_conv - 1, bsz, d), F32),
            pltpu.VMEM((n_layers, bsz, d), F32),
        ],
        compiler_params=pltpu.CompilerParams(
            dimension_semantics=("arbitrary",),
            vmem_limit_bytes=VMEM_LIMIT_BYTES),
        name="rglru_stack",
    )(x, mod, norm_g, w_in, conv_w, conv_b, w_ax, b_a, b_x, lam, w_out)


def _kv_kernel(x_ref, mod_ref, ng_ref, wk_ref, wvt_ref, k_ref, vt_ref, km_ref):
    d = x_ref.shape[2]
    n_heads = k_ref.shape[1]
    mod = mod_ref[0]
    h = _norm_modulate(x_ref[0], ng_ref[...], mod[:, :d], mod[:, d:]).astype(BF16)
    k = jnp.dot(h, wk_ref[...], preferred_element_type=F32)
    vt = lax.dot_general(wvt_ref[...], h, NT_DIMS, preferred_element_type=F32)
    for j in range(km_ref.shape[1]):
        km_ref[0, j] = jnp.mean(k[j * MOBA_BLOCK:(j + 1) * MOBA_BLOCK], axis=0, keepdims=True)
    pad_row = lax.broadcasted_iota(jnp.int32, (VT_ROWS - HEAD_DIM, vt.shape[1]), 0)
    ones_then_zeros = jnp.where(pad_row == 0, 1.0, 0.0).astype(BF16)
    for hd in range(n_heads):
        sl = slice(hd * HEAD_DIM, (hd + 1) * HEAD_DIM)
        k_ref[0, hd] = k[:, sl].astype(BF16)
        vt_ref[0, hd, :HEAD_DIM, :] = vt[sl, :].astype(BF16)
        vt_ref[0, hd, HEAD_DIM:, :] = ones_then_zeros


def _shared_kv(x, mod, norm_g, w_k, w_v_t, tile):
    bsz, seq, d = x.shape
    d_att = w_k.shape[1]
    n_heads = d_att // HEAD_DIM
    n_blk = seq // MOBA_BLOCK
    blk_per_tile = tile // MOBA_BLOCK
    assert tile % MOBA_BLOCK == 0 and seq % tile == 0
    return pl.pallas_call(
        _kv_kernel,
        grid=(bsz, seq // tile),
        in_specs=[
            pl.BlockSpec((1, tile, d), lambda b, i: (b, i, 0)),
            pl.BlockSpec((1, 1, 2 * d), lambda b, i: (b, 0, 0)),
            pl.BlockSpec((1, d), lambda b, i: (0, 0)),
            pl.BlockSpec((d, d_att), lambda b, i: (0, 0)),
            pl.BlockSpec((d_att, d), lambda b, i: (0, 0)),
        ],
        out_specs=[
            pl.BlockSpec((1, n_heads, tile, HEAD_DIM), lambda b, i: (b, 0, i, 0)),
            pl.BlockSpec((1, n_heads, VT_ROWS, tile), lambda b, i: (b, 0, 0, i)),
            pl.BlockSpec((1, blk_per_tile, 1, d_att), lambda b, i: (b, i, 0, 0)),
        ],
        out_shape=[jax.ShapeDtypeStruct((bsz, n_heads, seq, HEAD_DIM), BF16),
                   jax.ShapeDtypeStruct((bsz, n_heads, VT_ROWS, seq), BF16),
                   jax.ShapeDtypeStruct((bsz, n_blk, 1, d_att), F32)],
        compiler_params=pltpu.CompilerParams(
            dimension_semantics=("arbitrary", "arbitrary"),
            vmem_limit_bytes=VMEM_LIMIT_BYTES),
        name="shared_kv",
    )(x, mod, norm_g, w_k, w_v_t)


def _select_blocks(gate_t, n_past, sel_s):
    blk = lax.broadcasted_iota(jnp.int32, gate_t.shape, 0)
    g = jnp.where(blk < n_past, gate_t, NEG_INF)
    sel_s[...] = g
    rank = jnp.zeros(gate_t.shape, jnp.int32)
    for j in range(n_past):
        gj = jnp.broadcast_to(sel_s[j:j + 1, :], gate_t.shape)
        ahead = (gj > g) | ((gj == g) & (blk > j))
        rank = rank + ahead.astype(jnp.int32)
    sel_s[...] = jnp.where((rank < MOBA_TOPK) & (blk < n_past), 0.0, NEG_INF)


def _fold_rows(a, op):
    return op(a.reshape(a.shape[0] // SUBLANES, SUBLANES, a.shape[1]), axis=0)


MAX_HEADS_PER_STAGE = 8


def _heads_per_stage(n_past, slot_rows):
    per_stage = 1
    while (per_stage < MAX_HEADS_PER_STAGE
           and 2 * per_stage * (n_past + 1) * MOBA_BLOCK <= slot_rows):
        per_stage *= 2
    return per_stage


def _head_scores(hd, slot, lane, n_past, q_s, k_ref, km_ref, sel_s, sc_s, m_s):
    qh = q_s[hd]
    masked = n_past > MOBA_TOPK
    if masked:
        gate_t = lax.dot_general(km_ref[0, hd].astype(BF16), qh, NT_DIMS,
                                 preferred_element_type=F32)
        sel_s = sel_s.at[lane]
        _select_blocks(gate_t, n_past, sel_s)
    n_keys = (n_past + 1) * MOBA_BLOCK
    s_all = lax.dot_general(k_ref[0, hd, 0:n_keys, :], qh, NT_DIMS,
                            preferred_element_type=F32)
    m8 = None
    for j in range(n_past + 1):
        rows = slice(j * MOBA_BLOCK, (j + 1) * MOBA_BLOCK)
        s = s_all[rows]
        if j == n_past:
            key = lax.broadcasted_iota(jnp.int32, s.shape, 0)
            qry = lax.broadcasted_iota(jnp.int32, s.shape, 1)
            s = jnp.where(key <= qry, s, NEG_INF)
        elif masked:
            s = s + sel_s[j:j + 1, :]
        sc_s[slot, lane * n_keys + j * MOBA_BLOCK:lane * n_keys + (j + 1) * MOBA_BLOCK] = s
        smax = _fold_rows(s, jnp.max)
        m8 = smax if m8 is None else jnp.maximum(m8, smax)
    m_s[slot, lane] = m8


def _head_output(hd, slot, lane, n_past, vt_ref, sc_s, m_s, o_s):
    n_keys = (n_past + 1) * MOBA_BLOCK
    m = jnp.max(m_s[slot, lane], axis=0, keepdims=True)
    p = jnp.exp2(sc_s[slot, lane * n_keys:(lane + 1) * n_keys] - m).astype(BF16)
    o_t = jnp.dot(vt_ref[0, hd, :, 0:n_keys], p, preferred_element_type=F32)
    o_s[hd] = o_t[:HEAD_DIM] * (1.0 / o_t[HEAD_DIM:HEAD_DIM + 1])


def _moba_kernel(x_ref, mod_ref, ng_ref, win_ref, k_ref, vt_ref, km_ref, wout_ref, fg_ref,
                 o_ref, x_s, hb_s, q_s, g_s, o_s, sel_s, sc_s, m_s):
    d = x_ref.shape[2]
    n_layers = win_ref.shape[0]
    n_heads = k_ref.shape[1]
    n_blk = km_ref.shape[2]
    d_att = n_heads * HEAD_DIM
    qb = pl.program_id(1)
    scores = functools.partial(_head_scores, q_s=q_s, k_ref=k_ref, km_ref=km_ref, sel_s=sel_s,
                               sc_s=sc_s, m_s=m_s)
    output = functools.partial(_head_output, vt_ref=vt_ref, sc_s=sc_s, m_s=m_s, o_s=o_s)

    def project_q(lyr):
        mod = mod_ref[0, pl.ds(lyr, 1), :]
        hb = _norm_modulate(x_s[...], ng_ref[lyr], mod[:, :d], mod[:, d:2 * d]).astype(BF16)
        hb_s[...] = hb
        q = jnp.dot(hb, win_ref[lyr, :, :d_att], preferred_element_type=F32)
        q = (q * (HEAD_DIM ** -0.5 * LOG2E)).astype(BF16)
        for hd in range(n_heads):
            q_s[hd] = q[:, hd * HEAD_DIM:(hd + 1) * HEAD_DIM]

    def attend_and_project(n_past, lyr):
        per_stage = _heads_per_stage(n_past, sc_s.shape[1])
        n_groups = n_heads // per_stage
        assert sc_s.shape[0] >= min(n_groups, 2)

        def stage(fn, group):
            for lane in range(per_stage):
                fn(group * per_stage + lane, group & 1, lane, n_past)

        stage(scores, 0)
        g_s[...] = jnp.dot(hb_s[...], win_ref[lyr, :, d_att:], preferred_element_type=F32)

        def step(group, carry):
            stage(output, group - 1)
            stage(scores, group)
            return carry

        lax.fori_loop(1, n_groups, step, 0)
        stage(output, n_groups - 1)

        ys = []
        for hd in range(n_heads):
            gp = g_s[:, hd * HEAD_DIM:(hd + 1) * HEAD_DIM]
            ys.append((o_s[hd].T * (gp * _sigmoid(gp))).astype(BF16))
        proj = jnp.dot(jnp.concatenate(ys, axis=1), wout_ref[lyr], preferred_element_type=F32)
        gate = mod_ref[0, pl.ds(lyr, 1), 2 * d:]
        x_s[...] = x_s[...] + gate * proj

    x_s[...] = x_ref[0]

    def one_layer(lyr, carry):
        project_q(lyr)
        for n_past in range(n_blk):
            pl.when(qb == n_past)(functools.partial(attend_and_project, n_past, lyr))
        return carry

    lax.fori_loop(0, n_layers, one_layer, 0)
    out = x_s[...]
    ms = jnp.mean(out * out, axis=-1, keepdims=True)
    o_ref[0] = out * lax.rsqrt(ms + EPS) * fg_ref[...]


def _moba_stack(x, mod, norm_g, w_in, k, v_t, k_mean, w_out, final_g):
    bsz, seq, d = x.shape
    n_layers = w_in.shape[0]
    n_heads = k.shape[1]
    d_att = n_heads * HEAD_DIM
    n_blk = seq // MOBA_BLOCK
    full = lambda shape: pl.BlockSpec(shape, lambda b, i: (0,) * len(shape))
    per_batch = lambda shape, **kw: pl.BlockSpec(
        shape, lambda b, i: (b,) + (0,) * (len(shape) - 1), **kw)
    return pl.pallas_call(
        _moba_kernel,
        grid=(bsz, n_blk),
        in_specs=[
            pl.BlockSpec((1, MOBA_BLOCK, d), lambda b, i: (b, i, 0)),
            per_batch((1, n_layers, 3 * d)),
            full((n_layers, 1, d)),
            full((n_layers, d, 2 * d_att)),
            per_batch((1, n_heads, seq, HEAD_DIM)),
            per_batch((1, n_heads, VT_ROWS, seq), pipeline_mode=pl.Buffered(1)),
            per_batch((1, n_heads, n_blk, HEAD_DIM)),
            full((n_layers, d_att, d)),
            full((1, d)),
        ],
        out_specs=pl.BlockSpec((1, MOBA_BLOCK, d), lambda b, i: (b, i, 0)),
        out_shape=jax.ShapeDtypeStruct(x.shape, F32),
        scratch_shapes=[
            pltpu.VMEM((MOBA_BLOCK, d), F32),
            pltpu.VMEM((MOBA_BLOCK, d), BF16),
            pltpu.VMEM((n_heads, MOBA_BLOCK, HEAD_DIM), BF16),
            pltpu.VMEM((MOBA_BLOCK, d_att), F32),
            pltpu.VMEM((n_heads, HEAD_DIM, MOBA_BLOCK), F32),
            pltpu.VMEM((MAX_HEADS_PER_STAGE, n_blk, MOBA_BLOCK), F32),
            pltpu.VMEM((1, n_heads * seq, MOBA_BLOCK), F32),
            pltpu.VMEM((2, MAX_HEADS_PER_STAGE, SUBLANES, MOBA_BLOCK), F32),
        ],
        compiler_params=pltpu.CompilerParams(
            dimension_semantics=("arbitrary", "arbitrary"),
            vmem_limit_bytes=VMEM_LIMIT_BYTES),
        name="moba_stack",
    )(x, mod, norm_g, w_in, k, v_t, k_mean, w_out, final_g)


def kernel(x, c, mod_w, mod_b, norm_g, rg_w_in, rg_conv_w, rg_conv_b, rg_w_a, rg_b_a, rg_w_x,
           rg_b_x, rg_lambda, rg_w_out, kv_norm_g, kv_mod_w, kv_mod_b, w_kv, att_w_in,
           att_w_out, final_norm_g):
    bsz, seq, d = x.shape
    depth = mod_w.shape[0]
    n_a = rg_w_in.shape[0]
    n_b = att_w_in.shape[0]
    assert depth == n_a + n_b and seq % MOBA_BLOCK == 0
    d_att = w_kv.shape[1] // 2
    n_heads = d_att // HEAD_DIM
    n_blk = seq // MOBA_BLOCK

    mod = _modulation(c, mod_w, mod_b, tn=3 * d // 2)
    kv_mod = _modulation(c, kv_mod_w[None], kv_mod_b[None], tn=d)

    row = lambda p: p.reshape(1, -1)
    rows = lambda p: p[:, None, :]
    w_ax = jnp.concatenate([rg_w_a, rg_w_x], axis=-1).astype(BF16)
    x = _rglru_stack(
        x, mod[:n_a], rows(norm_g[:n_a]), rg_w_in.astype(BF16), rg_conv_w, rows(rg_conv_b), w_ax,
        rows(rg_b_a), rows(rg_b_x), rows(rg_lambda), rg_w_out.astype(BF16),
        t_len=RGLRU_TILE_STEPS)

    w_kv_b = w_kv.astype(BF16)
    k, v_t, k_mean = _shared_kv(x, kv_mod[0][:, None, :], row(kv_norm_g),
                                w_kv_b[:, :d_att], w_kv_b[:, d_att:].T, tile=KV_TILE_ROWS)
    k_mean = k_mean.reshape(bsz, n_blk, n_heads, HEAD_DIM).transpose(0, 2, 1, 3)

    return _moba_stack(
        x, mod[n_a:].transpose(1, 0, 2), norm_g[n_a:, None, :], att_w_in.astype(BF16), k, v_t,
        k_mean, att_w_out.astype(BF16), row(final_norm_g))
```

```python
import functools

import jax
import jax.numpy as jnp
from jax import lax
from jax.experimental import pallas as pl
from jax.experimental.pallas import tpu as pltpu

EPS = 1e-6
RG_C = 8.0
HEAD_DIM = 128
MOBA_BLOCK = 256
MOBA_TOPK = 3
NEG_INF = -1e30
LOG2E = 1.4426950408889634
SUBLANES = 8
BF16_SUBLANES = 16
MXU_DIM = 256
VT_ROWS = HEAD_DIM + BF16_SUBLANES
VMEM_LIMIT_BYTES = 56 * 1024 * 1024
RGLRU_TILE_STEPS = 128
KV_TILE_ROWS = 4 * MOBA_BLOCK

F32 = jnp.float32
BF16 = jnp.bfloat16
NT_DIMS = (((1,), (1,)), ((), ()))


def _sigmoid(z):
    return 1.0 / (1.0 + jnp.exp2(z * (-LOG2E)))


def _sqrt_nonneg(v):
    return jnp.where(v > 0.0, v * lax.rsqrt(v), 0.0)


def _norm_modulate(x, norm_g, shift, scale):
    ms = jnp.mean(x * x, axis=-1, keepdims=True)
    return x * lax.rsqrt(ms + EPS) * (norm_g * (1.0 + scale)) + shift


def _mod_kernel(c_ref, w_ref, b_ref, o_ref):
    c = c_ref[...]
    cs = (c * _sigmoid(c)).astype(BF16)
    w = w_ref[0].astype(BF16)
    o_ref[0] = jnp.dot(cs, w, preferred_element_type=F32) + b_ref[0]


def _modulation(c, w, b, tn):
    n_layers, d, n = w.shape
    bsz = c.shape[0]
    return pl.pallas_call(
        _mod_kernel,
        grid=(n_layers, n // tn),
        in_specs=[
            pl.BlockSpec((bsz, d), lambda l, j: (0, 0)),
            pl.BlockSpec((1, d, tn), lambda l, j: (l, 0, j)),
            pl.BlockSpec((1, 1, tn), lambda l, j: (l, 0, j)),
        ],
        out_specs=pl.BlockSpec((1, bsz, tn), lambda l, j: (l, 0, j)),
        out_shape=jax.ShapeDtypeStruct((n_layers, bsz, n), F32),
        compiler_params=pltpu.CompilerParams(
            dimension_semantics=("arbitrary", "arbitrary"),
            vmem_limit_bytes=VMEM_LIMIT_BYTES),
        name="adaln_mod",
    )(c, w, b.reshape(n_layers, 1, n))


def _tile_copies(hbm_ref, buf, sem, tile, slot, to_hbm):
    t_len, bsz, _ = buf.shape[1:]
    copies = []
    for b in range(bsz):
        hbm = hbm_ref.at[b, pl.ds(tile * t_len, t_len), :]
        vmem = buf.at[slot, :, b, :]
        src, dst = (vmem, hbm) if to_hbm else (hbm, vmem)
        copies.append(pltpu.make_async_copy(src, dst, sem.at[slot]))
    return copies


def _rglru_kernel(x_hbm, mod_ref, ng_ref, win_ref, cw_ref, cb_ref, wax_ref, ba_ref,
                  bx_ref, lam_ref, wout_ref, o_hbm,
                  xbuf, obuf, in_sem, out_sem, hs_s, y_s, utail, hstate):
    _, t_len, bsz, d = xbuf.shape
    n_layers, n_heads, rb, _ = wax_ref.shape
    conv_w = cw_ref.shape[1]
    i = pl.program_id(0)
    n_tiles = x_hbm.shape[1] // t_len
    slot = i & 1

    @pl.when(i == 0)
    def _():
        for c in _tile_copies(x_hbm, xbuf, in_sem, 0, 0, to_hbm=False):
            c.start()
        utail[...] = jnp.zeros_like(utail)
        hstate[...] = jnp.zeros_like(hstate)

    @pl.when(i + 1 < n_tiles)
    def _():
        for c in _tile_copies(x_hbm, xbuf, in_sem, i + 1, 1 - slot, to_hbm=False):
            c.start()

    @pl.when(i >= 2)
    def _():
        for c in _tile_copies(o_hbm, obuf, out_sem, i - 2, slot, to_hbm=True):
            c.wait()

    for c in _tile_copies(x_hbm, xbuf, in_sem, i, slot, to_hbm=False):
        c.wait()

    rows = t_len * bsz

    def layer(lyr, x3):
        mod = mod_ref[lyr]
        shift, scale, gate = mod[:, :d], mod[:, d:2 * d], mod[:, 2 * d:]
        lam = lam_ref[lyr]
        softplus_neg_lam = jnp.maximum(-lam, 0.0) + jnp.log1p(jnp.exp(-jnp.abs(lam)))
        log2_a_per_r = (-RG_C * LOG2E) * softplus_neg_lam

        ms = jnp.mean(x3 * x3, axis=-1, keepdims=True)
        h3 = x3 * lax.rsqrt(ms + EPS) * (ng_ref[lyr] * (1.0 + scale)) + shift
        hb = h3.reshape(rows, d).astype(BF16)
        u3 = jnp.dot(hb, win_ref[lyr, :, :d], preferred_element_type=F32).reshape(t_len, bsz, d)

        upad = jnp.concatenate([utail[lyr], u3], axis=0)
        utail[lyr] = u3[t_len - (conv_w - 1):]
        uc3 = cb_ref[lyr] + cw_ref[lyr, conv_w - 1:conv_w, :] * u3
        for k in range(conv_w - 1):
            uc3 = uc3 + cw_ref[lyr, k:k + 1, :] * upad[k:k + t_len]
        uc = uc3.reshape(rows, d)

        for hh in range(n_heads):
            sl = slice(hh * rb, (hh + 1) * rb)
            uch = uc[:, sl]
            z = jnp.dot(uch.astype(BF16), wax_ref[lyr, hh], preferred_element_type=F32)
            r = _sigmoid(z[:, :rb] + ba_ref[lyr, :, sl])
            gi = _sigmoid(z[:, rb:] + bx_ref[lyr, :, sl])
            a = jnp.exp2(r * log2_a_per_r[:, sl])
            b_in = _sqrt_nonneg(1.0 - a * a) * (gi * uch)
            a3 = a.reshape(t_len, bsz, rb)
            b3 = b_in.reshape(t_len, bsz, rb)
            h_run = hstate[lyr, :, sl]
            for t in range(t_len):
                h_run = a3[t] * h_run + b3[t]
                hs_s[t, :, sl] = h_run
            hstate[lyr, :, sl] = h_run
            gpath = jnp.dot(hb, win_ref[lyr, :, d + hh * rb:d + (hh + 1) * rb],
                            preferred_element_type=F32)
            y = hs_s[:, :, sl].reshape(rows, rb) * (gpath * _sigmoid(gpath))
            y_s[:, sl] = y.astype(BF16)

        proj = jnp.dot(y_s[...], wout_ref[lyr], preferred_element_type=F32)
        return x3 + gate * proj.reshape(t_len, bsz, d)

    x3 = xbuf[slot]
    for lyr in range(n_layers):
        x3 = layer(lyr, x3)
    obuf[slot] = x3

    for c in _tile_copies(o_hbm, obuf, out_sem, i, slot, to_hbm=True):
        c.start()

    @pl.when(i == n_tiles - 1)
    def _():
        if n_tiles >= 2:
            for c in _tile_copies(o_hbm, obuf, out_sem, i - 1, 1 - slot, to_hbm=True):
                c.wait()
        for c in _tile_copies(o_hbm, obuf, out_sem, i, slot, to_hbm=True):
            c.wait()


def _rglru_stack(x, mod, norm_g, w_in, conv_w, conv_b, w_ax, b_a, b_x, lam, w_out, t_len):
    bsz, seq, d = x.shape
    n_conv = conv_w.shape[1]
    assert bsz == SUBLANES and seq % t_len == 0 and t_len >= n_conv - 1
    n_layers = w_in.shape[0]
    full = lambda a: pl.BlockSpec(a.shape, lambda i: (0,) * a.ndim)
    return pl.pallas_call(
        _rglru_kernel,
        grid=(seq // t_len,),
        in_specs=[pl.BlockSpec(memory_space=pl.ANY)] + [
            full(a) for a in (mod, norm_g, w_in, conv_w, conv_b, w_ax, b_a, b_x, lam, w_out)],
        out_specs=pl.BlockSpec(memory_space=pl.ANY),
        out_shape=jax.ShapeDtypeStruct(x.shape, F32),
        scratch_shapes=[
            pltpu.VMEM((2, t_len, bsz, d), F32),
            pltpu.VMEM((2, t_len, bsz, d), F32),
            pltpu.SemaphoreType.DMA((2,)),
            pltpu.SemaphoreType.DMA((2,)),
            pltpu.VMEM((t_len, bsz, d), F32),
            pltpu.VMEM((t_len * bsz, d), BF16),
            pltpu.VMEM((n_layers, n_conv - 1, bsz, d), F32),
            pltpu.VMEM((n_layers, bsz, d), F32),
        ],
        compiler_params=pltpu.CompilerParams(
            dimension_semantics=("arbitrary",),
            vmem_limit_bytes=VMEM_LIMIT_BYTES),
        name="rglru_stack",
    )(x, mod, norm_g, w_in, conv_w, conv_b, w_ax, b_a, b_x, lam, w_out)


def _kv_kernel(x_ref, mod_ref, ng_ref, wkv_ref, k_ref, vt_ref, km_ref):
    d = x_ref.shape[2]
    n_heads = k_ref.shape[1]
    d_att = n_heads * HEAD_DIM
    mod = mod_ref[0]
    h = _norm_modulate(x_ref[0], ng_ref[...], mod[:, :d], mod[:, d:]).astype(BF16)
    kv = jnp.dot(h, wkv_ref[...], preferred_element_type=F32)
    k = kv[:, :d_att]
    for j in range(km_ref.shape[1]):
        km_ref[0, j] = jnp.mean(k[j * MOBA_BLOCK:(j + 1) * MOBA_BLOCK], axis=0, keepdims=True)
    pad_row = lax.broadcasted_iota(jnp.int32, (VT_ROWS - HEAD_DIM, kv.shape[0]), 0)
    ones_then_zeros = jnp.where(pad_row == 0, 1.0, 0.0).astype(BF16)
    for hd in range(n_heads):
        sl = slice(hd * HEAD_DIM, (hd + 1) * HEAD_DIM)
        k_ref[0, hd] = k[:, sl].astype(BF16)
        v_h = kv[:, d_att + hd * HEAD_DIM:d_att + (hd + 1) * HEAD_DIM]
        vt_ref[0, hd, :HEAD_DIM, :] = v_h.T.astype(BF16)
        vt_ref[0, hd, HEAD_DIM:, :] = ones_then_zeros


def _shared_kv(x, mod, norm_g, w_kv, tile):
    bsz, seq, d = x.shape
    d_att = w_kv.shape[1] // 2
    n_heads = d_att // HEAD_DIM
    n_blk = seq // MOBA_BLOCK
    blk_per_tile = tile // MOBA_BLOCK
    assert tile % MOBA_BLOCK == 0 and seq % tile == 0
    return pl.pallas_call(
        _kv_kernel,
        grid=(bsz, seq // tile),
        in_specs=[
            pl.BlockSpec((1, tile, d), lambda b, i: (b, i, 0)),
            pl.BlockSpec((1, 1, 2 * d), lambda b, i: (b, 0, 0)),
            pl.BlockSpec((1, d), lambda b, i: (0, 0)),
            pl.BlockSpec((d, 2 * d_att), lambda b, i: (0, 0)),
        ],
        out_specs=[
            pl.BlockSpec((1, n_heads, tile, HEAD_DIM), lambda b, i: (b, 0, i, 0)),
            pl.BlockSpec((1, n_heads, VT_ROWS, tile), lambda b, i: (b, 0, 0, i)),
            pl.BlockSpec((1, blk_per_tile, 1, d_att), lambda b, i: (b, i, 0, 0)),
        ],
        out_shape=[jax.ShapeDtypeStruct((bsz, n_heads, seq, HEAD_DIM), BF16),
                   jax.ShapeDtypeStruct((bsz, n_heads, VT_ROWS, seq), BF16),
                   jax.ShapeDtypeStruct((bsz, n_blk, 1, d_att), F32)],
        compiler_params=pltpu.CompilerParams(
            dimension_semantics=("arbitrary", "arbitrary"),
            vmem_limit_bytes=VMEM_LIMIT_BYTES),
        name="shared_kv",
    )(x, mod, norm_g, w_kv)


def _select_blocks(gate_t, n_past, sel_s):
    blk = lax.broadcasted_iota(jnp.int32, gate_t.shape, 0)
    g = jnp.where(blk < n_past, gate_t, NEG_INF)
    sel_s[...] = g
    rank = jnp.zeros(gate_t.shape, jnp.int32)
    for j in range(n_past):
        gj = jnp.broadcast_to(sel_s[j:j + 1, :], gate_t.shape)
        ahead = (gj > g) | ((gj == g) & (blk > j))
        rank = rank + ahead.astype(jnp.int32)
    sel_s[...] = jnp.where((rank < MOBA_TOPK) & (blk < n_past), 0.0, NEG_INF)


def _fold_rows(a, op):
    return op(a.reshape(a.shape[0] // SUBLANES, SUBLANES, a.shape[1]), axis=0)


MAX_HEADS_PER_STAGE = 8


def _heads_per_stage(n_past, slot_rows):
    per_stage = 1
    while (per_stage < MAX_HEADS_PER_STAGE
           and 2 * per_stage * (n_past + 1) * MOBA_BLOCK <= slot_rows):
        per_stage *= 2
    return per_stage


def _head_scores(hd, slot, lane, n_past, q_s, k_ref, km_ref, sel_s, sc_s, m_s):
    qh = q_s[hd]
    masked = n_past > MOBA_TOPK
    if masked:
        gate_t = lax.dot_general(km_ref[0, hd].astype(BF16), qh, NT_DIMS,
                                 preferred_element_type=F32)
        sel_s = sel_s.at[lane]
        _select_blocks(gate_t, n_past, sel_s)
    n_keys = (n_past + 1) * MOBA_BLOCK
    s_all = lax.dot_general(k_ref[0, hd, 0:n_keys, :], qh, NT_DIMS,
                            preferred_element_type=F32)
    m8 = None
    for j in range(n_past + 1):
        rows = slice(j * MOBA_BLOCK, (j + 1) * MOBA_BLOCK)
        s = s_all[rows]
        if j == n_past:
            key = lax.broadcasted_iota(jnp.int32, s.shape, 0)
            qry = lax.broadcasted_iota(jnp.int32, s.shape, 1)
            s = jnp.where(key <= qry, s, NEG_INF)
        elif masked:
            s = s + sel_s[j:j + 1, :]
        sc_s[slot, lane * n_keys + j * MOBA_BLOCK:lane * n_keys + (j + 1) * MOBA_BLOCK] = s
        smax = _fold_rows(s, jnp.max)
        m8 = smax if m8 is None else jnp.maximum(m8, smax)
    m_s[slot, lane] = m8


def _head_output(hd, slot, lane, n_past, vt_ref, sc_s, m_s, o_s):
    n_keys = (n_past + 1) * MOBA_BLOCK
    m = jnp.max(m_s[slot, lane], axis=0, keepdims=True)
    p = jnp.exp2(sc_s[slot, lane * n_keys:(lane + 1) * n_keys] - m).astype(BF16)
    o_t = jnp.dot(vt_ref[0, hd, :, 0:n_keys], p, preferred_element_type=F32)
    o_s[hd] = o_t[:HEAD_DIM] * (1.0 / o_t[HEAD_DIM:HEAD_DIM + 1])


def _moba_kernel(x_ref, mod_ref, ng_ref, win_ref, k_ref, vt_ref, km_ref, wout_ref, fg_ref,
                 o_ref, x_s, hb_s, q_s, g_s, o_s, sel_s, sc_s, m_s):
    d = x_ref.shape[2]
    n_layers = win_ref.shape[0]
    n_heads = k_ref.shape[1]
    n_blk = km_ref.shape[2]
    d_att = n_heads * HEAD_DIM
    qb = pl.program_id(1)
    scores = functools.partial(_head_scores, q_s=q_s, k_ref=k_ref, km_ref=km_ref, sel_s=sel_s,
                               sc_s=sc_s, m_s=m_s)
    output = functools.partial(_head_output, vt_ref=vt_ref, sc_s=sc_s, m_s=m_s, o_s=o_s)

    def project_q(lyr):
        mod = mod_ref[0, pl.ds(lyr, 1), :]
        hb = _norm_modulate(x_s[...], ng_ref[lyr], mod[:, :d], mod[:, d:2 * d]).astype(BF16)
        hb_s[...] = hb
        q = jnp.dot(hb, win_ref[lyr, :, :d_att], preferred_element_type=F32)
        q = (q * (HEAD_DIM ** -0.5 * LOG2E)).astype(BF16)
        for hd in range(n_heads):
            q_s[hd] = q[:, hd * HEAD_DIM:(hd + 1) * HEAD_DIM]

    def attend_and_project(n_past, lyr):
        per_stage = _heads_per_stage(n_past, sc_s.shape[1])
        n_groups = n_heads // per_stage
        assert sc_s.shape[0] >= min(n_groups, 2)

        def stage(fn, group):
            for lane in range(per_stage):
                fn(group * per_stage + lane, group & 1, lane, n_past)

        stage(scores, 0)
        g_s[...] = jnp.dot(hb_s[...], win_ref[lyr, :, d_att:], preferred_element_type=F32)

        def step(group, carry):
            stage(output, group - 1)
            stage(scores, group)
            return carry

        lax.fori_loop(1, n_groups, step, 0)
        stage(output, n_groups - 1)

        ys = []
        for hd in range(n_heads):
            gp = g_s[:, hd * HEAD_DIM:(hd + 1) * HEAD_DIM]
            ys.append((o_s[hd].T * (gp * _sigmoid(gp))).astype(BF16))
        proj = jnp.dot(jnp.concatenate(ys, axis=1), wout_ref[lyr], preferred_element_type=F32)
        gate = mod_ref[0, pl.ds(lyr, 1), 2 * d:]
        x_s[...] = x_s[...] + gate * proj

    x_s[...] = x_ref[0]

    def one_layer(lyr, carry):
        project_q(lyr)
        for n_past in range(n_blk):
            pl.when(qb == n_past)(functools.partial(attend_and_project, n_past, lyr))
        return carry

    lax.fori_loop(0, n_layers, one_layer, 0)
    out = x_s[...]
    ms = jnp.mean(out * out, axis=-1, keepdims=True)
    o_ref[0] = out * lax.rsqrt(ms + EPS) * fg_ref[...]


def _moba_stack(x, mod, norm_g, w_in, k, v_t, k_mean, w_out, final_g):
    bsz, seq, d = x.shape
    n_layers = w_in.shape[0]
    n_heads = k.shape[1]
    d_att = n_heads * HEAD_DIM
    n_blk = seq // MOBA_BLOCK
    full = lambda shape: pl.BlockSpec(shape, lambda b, i: (0,) * len(shape))
    per_batch = lambda shape, **kw: pl.BlockSpec(
        shape, lambda b, i: (b,) + (0,) * (len(shape) - 1), **kw)
    return pl.pallas_call(
        _moba_kernel,
        grid=(bsz, n_blk),
        in_specs=[
            pl.BlockSpec((1, MOBA_BLOCK, d), lambda b, i: (b, i, 0)),
            per_batch((1, n_layers, 3 * d)),
            full((n_layers, 1, d)),
            full((n_layers, d, 2 * d_att)),
            per_batch((1, n_heads, seq, HEAD_DIM)),
            per_batch((1, n_heads, VT_ROWS, seq), pipeline_mode=pl.Buffered(1)),
            per_batch((1, n_heads, n_blk, HEAD_DIM)),
            full((n_layers, d_att, d)),
            full((1, d)),
        ],
        out_specs=pl.BlockSpec((1, MOBA_BLOCK, d), lambda b, i: (b, i, 0)),
        out_shape=jax.ShapeDtypeStruct(x.shape, F32),
        scratch_shapes=[
            pltpu.VMEM((MOBA_BLOCK, d), F32),
            pltpu.VMEM((MOBA_BLOCK, d), BF16),
            pltpu.VMEM((n_heads, MOBA_BLOCK, HEAD_DIM), BF16),
            pltpu.VMEM((MOBA_BLOCK, d_att), F32),
            pltpu.VMEM((n_heads, HEAD_DIM, MOBA_BLOCK), F32),
            pltpu.VMEM((MAX_HEADS_PER_STAGE, n_blk, MOBA_BLOCK), F32),
            pltpu.VMEM((1, n_heads * seq, MOBA_BLOCK), F32),
            pltpu.VMEM((2, MAX_HEADS_PER_STAGE, SUBLANES, MOBA_BLOCK), F32),
        ],
        compiler_params=pltpu.CompilerParams(
            dimension_semantics=("arbitrary", "arbitrary"),
            vmem_limit_bytes=VMEM_LIMIT_BYTES),
        name="moba_stack",
    )(x, mod, norm_g, w_in, k, v_t, k_mean, w_out, final_g)


def kernel(x, c, mod_w, mod_b, norm_g, rg_w_in, rg_conv_w, rg_conv_b, rg_w_a, rg_b_a, rg_w_x,
           rg_b_x, rg_lambda, rg_w_out, kv_norm_g, kv_mod_w, kv_mod_b, w_kv, att_w_in,
           att_w_out, final_norm_g):
    bsz, seq, d = x.shape
    depth = mod_w.shape[0]
    n_a = rg_w_in.shape[0]
    n_b = att_w_in.shape[0]
    assert depth == n_a + n_b and seq % MOBA_BLOCK == 0
    d_att = w_kv.shape[1] // 2
    n_heads = d_att // HEAD_DIM
    n_blk = seq // MOBA_BLOCK

    mod = _modulation(c, mod_w, mod_b, tn=3 * d // 2)
    kv_mod = _modulation(c, kv_mod_w[None], kv_mod_b[None], tn=d)

    row = lambda p: p.reshape(1, -1)
    rows = lambda p: p[:, None, :]
    w_ax = jnp.concatenate([rg_w_a, rg_w_x], axis=-1).astype(BF16)
    x = _rglru_stack(
        x, mod[:n_a], rows(norm_g[:n_a]), rg_w_in.astype(BF16), rg_conv_w, rows(rg_conv_b), w_ax,
        rows(rg_b_a), rows(rg_b_x), rows(rg_lambda), rg_w_out.astype(BF16),
        t_len=RGLRU_TILE_STEPS)

    k, v_t, k_mean = _shared_kv(x, kv_mod[0][:, None, :], row(kv_norm_g), w_kv.astype(BF16),
                                tile=KV_TILE_ROWS)
    k_mean = k_mean.reshape(bsz, n_blk, n_heads, HEAD_DIM).transpose(0, 2, 1, 3)

    return _moba_stack(
        x, mod[n_a:].transpose(1, 0, 2), norm_g[n_a:, None, :], att_w_in.astype(BF16), k, v_t,
        k_mean, att_w_out.astype(BF16), row(final_norm_g))
```
